```python
import jax, jax.numpy as jnp
from jax import lax
import numpy as np

D_MODEL = 2048
BATCH = 4
SEQ = 2048
DEPTH = 4
DEC_BATCH = 128
DEC_SEQ = 4
PAST_LEN = 16384
PAGE_SIZE = 128

N_MIXERS = 3
N_A_LAYERS = (DEPTH + 2) // 3
N_B_LAYERS = (DEPTH + 1) // 3
N_C_LAYERS = DEPTH // 3
CONV_A_WIDTH = 31
CONV_B_WIDTH = 3
FFN_CONV_WIDTH = 3
CHUNK = 128
D_C = D_MODEL
C_GROUPS = 8
C_GROUP_DIM = D_C // C_GROUPS
D_FF = ((8 * D_MODEL // 3 + 255) // 256) * 256
PLE_DIM = 256
EPS = 1e-6

kernel_name = "interleaved_conv_gmlp_decoder_step"


def rmsnorm(x, g):
    xf = x.astype(jnp.float32)
    y = xf * lax.rsqrt(jnp.mean(xf * xf, axis=-1, keepdims=True) + EPS)
    return (y * g.astype(jnp.float32)).astype(x.dtype)


def layernorm(x, g, b):
    xf = x.astype(jnp.float32)
    mu = jnp.mean(xf, axis=-1, keepdims=True)
    xc = xf - mu
    y = xc * lax.rsqrt(jnp.mean(xc * xc, axis=-1, keepdims=True) + EPS)
    return (y * g.astype(jnp.float32) + b.astype(jnp.float32)).astype(x.dtype)


def causal_dwconv(x, buf, w):
    width = w.shape[0]
    xp = jnp.concatenate([buf.astype(x.dtype), x], axis=1)
    y = lax.conv_general_dilated(
        xp, w[:, None, :].astype(x.dtype), window_strides=(1,), padding="VALID",
        dimension_numbers=("NWC", "WIO", "NWC"), feature_group_count=x.shape[-1])
    return y, xp[:, xp.shape[1] - (width - 1):]


def conformer_conv(xn, buf, w_in, b_in, w_dw, b_dw, ln_g, ln_b, w_out, b_out):
    a, g = jnp.split(xn @ w_in + b_in, 2, axis=-1)
    z = a * jax.nn.sigmoid(g)
    z, new_buf = causal_dwconv(z, buf, w_dw)
    z = jax.nn.silu(layernorm(z + b_dw, ln_g, ln_b))
    return z @ w_out + b_out, new_buf


def short_conv(xn, buf, w_in, w_dw, w_out):
    gb, gc, xt = jnp.split(xn @ w_in, 3, axis=-1)
    z, new_buf = causal_dwconv(gc * xt, buf, w_dw)
    return (gb * z) @ w_out, new_buf


def chunk_gmlp(xn, w_in, b_in, ln_g, ln_b, w_s, b_s, w_out, b_out):
    nb, t, _ = xn.shape
    u, v = jnp.split(jax.nn.gelu(xn @ w_in + b_in), 2, axis=-1)
    v = layernorm(v, ln_g, ln_b)
    pad = (-t) % CHUNK
    vp = jnp.pad(v, ((0, 0), (0, pad), (0, 0)))
    n_chunks = (t + pad) // CHUNK
    vc = vp.reshape(nb, n_chunks, CHUNK, C_GROUPS, C_GROUP_DIM)
    ws = w_s * jnp.tril(jnp.ones((CHUNK, CHUNK), dtype=w_s.dtype))
    z = jnp.einsum("gij,bnjgd->bnigd", ws, vc) + b_s.T[None, None, :, :, None]
    z = z.reshape(nb, n_chunks * CHUNK, D_C)[:, :t]
    last_start = ((t - 1) // CHUNK) * CHUNK
    return (u * z) @ w_out + b_out, v[:, last_start:]


def conv_ffn(xn, buf, w_gate, w_dw, b_dw, w_up, w_down):
    g, new_buf = causal_dwconv(xn @ w_gate, buf, w_dw)
    h = jax.nn.silu(g + b_dw) * (xn @ w_up)
    return h @ w_down, new_buf


def setup_inputs(seed: int = 0) -> dict:
    key = jax.random.key(seed)
    ks = iter(jax.random.split(key, 48))

    def nrm(shape, scale=1.0):
        return jax.random.normal(next(ks), shape, jnp.float32) * scale

    def gain(shape):
        return 1.0 + nrm(shape, 0.05)

    d = D_MODEL
    return {
        "x_prompt": nrm((BATCH, SEQ, d)),
        "x_sample": nrm((DEC_BATCH, DEC_SEQ, d)),
        "p_prompt": nrm((DEPTH, BATCH, SEQ, PLE_DIM)),
        "p_sample": nrm((DEPTH, DEC_BATCH, DEC_SEQ, PLE_DIM)),
        "state_conformer": nrm((N_A_LAYERS, DEC_BATCH, CONV_A_WIDTH - 1, d), 0.5),
        "state_shortconv": nrm((N_B_LAYERS, DEC_BATCH, CONV_B_WIDTH - 1, d), 0.5),
        "state_ffn": nrm((DEPTH, DEC_BATCH, FFN_CONV_WIDTH - 1, D_FF), 0.5),
        "g_mix": gain((DEPTH, d)),
        "g_ffn": gain((DEPTH, d)),
        "g_ple": gain((DEPTH, d)),
        "g_final": gain((d,)),
        "a_w_in": nrm((N_A_LAYERS, d, 2 * d), d ** -0.5),
        "a_b_in": nrm((N_A_LAYERS, 2 * d), 0.02),
        "a_w_dw": nrm((N_A_LAYERS, CONV_A_WIDTH, d), CONV_A_WIDTH ** -0.5),
        "a_b_dw": nrm((N_A_LAYERS, d), 0.02),
        "a_ln_g": gain((N_A_LAYERS, d)),
        "a_ln_b": nrm((N_A_LAYERS, d), 0.02),
        "a_w_out": nrm((N_A_LAYERS, d, d), d ** -0.5),
        "a_b_out": nrm((N_A_LAYERS, d), 0.02),
        "b_w_in": nrm((N_B_LAYERS, d, 3 * d), d ** -0.5),
        "b_w_dw": nrm((N_B_LAYERS, CONV_B_WIDTH, d), CONV_B_WIDTH ** -0.5),
        "b_w_out": nrm((N_B_LAYERS, d, d), d ** -0.5),
        "c_w_in": nrm((N_C_LAYERS, d, 2 * D_C), d ** -0.5),
        "c_b_in": nrm((N_C_LAYERS, 2 * D_C), 0.02),
        "c_ln_g": gain((N_C_LAYERS, D_C)),
        "c_ln_b": nrm((N_C_LAYERS, D_C), 0.02),
        "c_w_s": nrm((N_C_LAYERS, C_GROUPS, CHUNK, CHUNK), CHUNK ** -0.5),
        "c_b_s": 1.0 + nrm((N_C_LAYERS, C_GROUPS, CHUNK), 0.1),
        "c_w_out": nrm((N_C_LAYERS, D_C, d), D_C ** -0.5),
        "c_b_out": nrm((N_C_LAYERS, d), 0.02),
        "f_w_gate": nrm((DEPTH, d, D_FF), d ** -0.5),
        "f_w_dw": nrm((DEPTH, FFN_CONV_WIDTH, D_FF), FFN_CONV_WIDTH ** -0.5),
        "f_b_dw": nrm((DEPTH, D_FF), 0.02),
        "f_w_up": nrm((DEPTH, d, D_FF), d ** -0.5),
        "f_w_down": nrm((DEPTH, D_FF, d), D_FF ** -0.5),
        "ple_w_proj": nrm((DEPTH, PLE_DIM, d), PLE_DIM ** -0.5),
        "ple_w_gate": nrm((DEPTH, d, d), d ** -0.5),
    }


def reference(x_prompt, x_sample, p_prompt, p_sample, state_conformer, state_shortconv,
              state_ffn, g_mix, g_ffn, g_ple, g_final,
              a_w_in, a_b_in, a_w_dw, a_b_dw, a_ln_g, a_ln_b, a_w_out, a_b_out,
              b_w_in, b_w_dw, b_w_out,
              c_w_in, c_b_in, c_ln_g, c_ln_b, c_w_s, c_b_s, c_w_out, c_b_out,
              f_w_gate, f_w_dw, f_b_dw, f_w_up, f_w_down,
              ple_w_proj, ple_w_gate):

    def trunk(x, p, st_a, st_b, st_f):
        h = x
        new_a, new_b, new_c, new_f = [], [], [], []
        for i in range(DEPTH):
            kind, j = i % N_MIXERS, i // N_MIXERS
            xn = rmsnorm(h, g_mix[i])
            if kind == 0:
                out, nb = conformer_conv(xn, st_a[j], a_w_in[j], a_b_in[j], a_w_dw[j], a_b_dw[j],
                                         a_ln_g[j], a_ln_b[j], a_w_out[j], a_b_out[j])
                new_a.append(nb)
            elif kind == 1:
                out, nb = short_conv(xn, st_b[j], b_w_in[j], b_w_dw[j], b_w_out[j])
                new_b.append(nb)
            else:
                out, nv = chunk_gmlp(xn, c_w_in[j], c_b_in[j], c_ln_g[j], c_ln_b[j],
                                     c_w_s[j], c_b_s[j], c_w_out[j], c_b_out[j])
                new_c.append(nv)
            h = h + out
            out, nf = conv_ffn(rmsnorm(h, g_ffn[i]), st_f[i], f_w_gate[i], f_w_dw[i],
                               f_b_dw[i], f_w_up[i], f_w_down[i])
            new_f.append(nf)
            h = h + out
            gate = jax.nn.sigmoid(rmsnorm(h, g_ple[i]) @ ple_w_gate[i])
            h = h + gate * (p[i] @ ple_w_proj[i])
        return (rmsnorm(h, g_final), jnp.stack(new_a), jnp.stack(new_b),
                jnp.stack(new_c), jnp.stack(new_f))

    bp = x_prompt.shape[0]
    dt = x_prompt.dtype
    zero_a = jnp.zeros((N_A_LAYERS, bp, CONV_A_WIDTH - 1, D_MODEL), dt)
    zero_b = jnp.zeros((N_B_LAYERS, bp, CONV_B_WIDTH - 1, D_MODEL), dt)
    zero_f = jnp.zeros((DEPTH, bp, FFN_CONV_WIDTH - 1, D_FF), dt)

    y_prompt, conf_p, sc_p, cv_p, ffn_p = trunk(x_prompt, p_prompt, zero_a, zero_b, zero_f)
    y_sample, conf_s, sc_s, cv_s, ffn_s = trunk(x_sample, p_sample, state_conformer,
                                                state_shortconv, state_ffn)
    return (y_prompt, y_sample, conf_p, conf_s, sc_p, sc_s, cv_p, cv_s, ffn_p, ffn_s)
```

```python
import functools

import jax
import jax.numpy as jnp
from jax import lax
from jax.experimental import pallas as pl
from jax.experimental.pallas import tpu as pltpu

EPS = 1e-6
F32 = jnp.float32
BF16 = jnp.bfloat16

ROW_TILE = 512
COL_TILE = 512
CONV_COL_TILE = 256
SUBLANES = 8
VMEM_LIMIT = 56 * 1024 * 1024


def _params(*sem):
    return pltpu.CompilerParams(dimension_semantics=sem, vmem_limit_bytes=VMEM_LIMIT)


def _rmsnorm(x, g):
    return x * lax.rsqrt(jnp.mean(x * x, axis=-1, keepdims=True) + EPS) * g


def _layernorm(x, g, b):
    xc = x - jnp.mean(x, axis=-1, keepdims=True)
    return xc * lax.rsqrt(jnp.mean(xc * xc, axis=-1, keepdims=True) + EPS) * g + b


def _dot(a, b):
    return jnp.dot(a, b, preferred_element_type=F32)


def _round_up(n, m):
    return -(-n // m) * m


def _conv_rows(ext_ref, w_ref, n_rows, hist_rows, row_chunk):
    width = w_ref.shape[0]
    base = hist_rows - (width - 1)
    n_cols = ext_ref.shape[1]
    out = []
    for r0 in range(0, n_rows, row_chunk):
        cols = []
        for c0 in range(0, n_cols, 128):
            acc = None
            for k in range(width):
                term = w_ref[k:k + 1, c0:c0 + 128] * ext_ref[base + k + r0:base + k + r0 + row_chunk, c0:c0 + 128]
                acc = term if acc is None else acc + term
            cols.append(acc)
        out.append(jnp.concatenate(cols, axis=1) if len(cols) > 1 else cols[0])
    return jnp.concatenate(out, axis=0) if len(out) > 1 else out[0]


def _conv_slabs(slabs, w_ref, n_out):
    width = w_ref.shape[0]
    out = []
    for t in range(n_out):
        acc = None
        for k in range(width):
            term = w_ref[k:k + 1, :] * slabs[t + k]
            acc = term if acc is None else acc + term
        out.append(acc)
    return out


def _in_proj_kernel(h_ref, g_ref, w0_ref, w1_ref, b0_ref, b1_ref, *rest, kind):
    outs, xn_ref = rest[:-1], rest[-1]

    @pl.when(pl.program_id(1) == 0)
    def _():
        xn_ref[...] = _rmsnorm(h_ref[...], g_ref[...]).astype(BF16)

    xn = xn_ref[...]
    a = _dot(xn, w0_ref[...]) + b0_ref[...]
    b = _dot(xn, w1_ref[...]) + b1_ref[...]
    if kind == "glu":
        outs[0][...] = a * jax.nn.sigmoid(b)
    else:
        outs[0][...] = jax.nn.gelu(a)
        outs[1][...] = jax.nn.gelu(b)


def _in_proj(h, g, g_layer, w, b, layer, kind):
    m, d = h.shape
    tm, tn = min(ROW_TILE, m), min(COL_TILE, d)
    nj = d // tn
    n_out = 1 if kind == "glu" else 2
    return pl.pallas_call(
        functools.partial(_in_proj_kernel, kind=kind),
        grid=(m // tm, nj),
        in_specs=[
            pl.BlockSpec((tm, d), lambda i, j: (i, 0)),
            pl.BlockSpec((None, 1, d), lambda i, j: (g_layer, 0, 0)),
            pl.BlockSpec((None, d, tn), lambda i, j: (layer, 0, j)),
            pl.BlockSpec((None, d, tn), lambda i, j: (layer, 0, j + nj)),
            pl.BlockSpec((None, 1, tn), lambda i, j: (layer, 0, j)),
            pl.BlockSpec((None, 1, tn), lambda i, j: (layer, 0, j + nj)),
        ],
        out_specs=[pl.BlockSpec((tm, tn), lambda i, j: (i, j))] * n_out,
        out_shape=[jax.ShapeDtypeStruct((m, d), F32)] * n_out,
        scratch_shapes=[pltpu.VMEM((tm, d), BF16)],
        compiler_params=_params("parallel", "arbitrary"),
        name=f"in_proj_{kind}",
    )(h, g, w, w, b, b)


def _gated_combine(proj, conv_fn, bias, mode):
    if mode == "ffn":
        gate_in, up = proj
        return jax.nn.silu(conv_fn(gate_in) + bias) * up, gate_in
    gb, gc, xt = proj
    v = gc * xt
    return gb * conv_fn(v), v


def _gated_prompt_kernel(*refs, n_in, mode, tiles_per_seq):
    h_ref, g_ref = refs[:2]
    w_in = refs[2:2 + n_in]
    wdw_ref, bdw_ref, w2_ref, ho_ref, tail_ref, xn_ref, acc_ref, ext_ref, carry_ref = refs[2 + n_in:]
    i, j = pl.program_id(0), pl.program_id(1)
    tm = h_ref.shape[0]
    hist = SUBLANES

    @pl.when(j == 0)
    def _():
        xn_ref[...] = _rmsnorm(h_ref[...], g_ref[...]).astype(BF16)

    xn = xn_ref[...]
    proj = [_dot(xn, w[...]) for w in w_in]
    first = (i % tiles_per_seq) == 0

    def conv_fn(v):
        @pl.when(first)
        def _():
            ext_ref[0:hist, :] = jnp.zeros((hist, v.shape[1]), F32)

        @pl.when(jnp.logical_not(first))
        def _():
            ext_ref[0:hist, :] = carry_ref[j]

        ext_ref[hist:hist + tm, :] = v
        carry_ref[j] = v[tm - hist:, :]
        tail_ref[...] = v[tm - hist:, :]
        return _conv_rows(ext_ref, wdw_ref, tm, hist, tm)

    hid, _ = _gated_combine(proj, conv_fn, bdw_ref[...], mode)
    contrib = _dot(hid.astype(BF16), w2_ref[...])

    @pl.when(j == 0)
    def _():
        acc_ref[...] = contrib

    @pl.when(j > 0)
    def _():
        acc_ref[...] += contrib

    @pl.when(j == pl.num_programs(1) - 1)
    def _():
        ho_ref[...] = h_ref[...] + acc_ref[...]


def _gated_sample_kernel(*refs, n_in, mode, n_steps):
    h_ref, g_ref = refs[:2]
    w_in = refs[2:2 + n_in]
    wdw_ref, bdw_ref, w2_ref, st_ref, ho_ref, ns_ref, xn_ref, acc_ref = refs[2 + n_in:]
    j = pl.program_id(0)
    n_hist = st_ref.shape[0]
    rb = h_ref.shape[0] // n_steps

    @pl.when(j == 0)
    def _():
        xn_ref[...] = _rmsnorm(h_ref[...], g_ref[...]).astype(BF16)

    xn = xn_ref[...]
    proj = [_dot(xn, w[...]) for w in w_in]

    def conv_fn(v):
        slabs = [st_ref[k] for k in range(n_hist)] + [v[t * rb:(t + 1) * rb, :] for t in range(n_steps)]
        for k in range(n_hist):
            ns_ref[k] = slabs[n_steps + k]
        return jnp.concatenate(_conv_slabs(slabs, wdw_ref, n_steps), axis=0)

    hid, _ = _gated_combine(proj, conv_fn, bdw_ref[...], mode)
    contrib = _dot(hid.astype(BF16), w2_ref[...])

    @pl.when(j == 0)
    def _():
        acc_ref[...] = contrib

    @pl.when(j > 0)
    def _():
        acc_ref[...] += contrib

    @pl.when(j == pl.num_programs(0) - 1)
    def _():
        ho_ref[...] = h_ref[...] + acc_ref[...]


def _gated(h, g, g_layer, w_in_list, wdw, bdw, w2, layer, mode, *, seq_len=None, state=None, n_steps=None):
    m, d = h.shape
    hidden = w2.shape[1]
    tf = min(COL_TILE, hidden)
    nj = hidden // tf
    width = wdw.shape[1]
    n_in = len(w_in_list)
    sample = state is not None
    tm = m if sample else min(ROW_TILE, seq_len)
    if sample:
        row = lambda j: (0, 0)
        col = lambda off: (lambda j: (layer, 0, j + off))
        vec = lambda l: (lambda j: (l, 0, 0))
        hid_vec = lambda j: (layer, 0, j)
        w2_map = lambda j: (layer, j, 0)
        grid = (nj,)
    else:
        row = lambda i, j: (i, 0)
        col = lambda off: (lambda i, j: (layer, 0, j + off))
        vec = lambda l: (lambda i, j: (l, 0, 0))
        hid_vec = lambda i, j: (layer, 0, j)
        w2_map = lambda i, j: (layer, j, 0)
        grid = (m // tm, nj)
    in_specs = [pl.BlockSpec((tm, d), row), pl.BlockSpec((None, 1, d), vec(g_layer))]
    in_specs += [pl.BlockSpec((None, d, tf), col(off * nj)) for _, off in w_in_list]
    in_specs += [
        pl.BlockSpec((None, width, tf), hid_vec),
        pl.BlockSpec((None, 1, tf), hid_vec),
        pl.BlockSpec((None, tf, d), w2_map),
    ]
    args = [h, g] + [w for w, _ in w_in_list] + [wdw, bdw, w2]
    scratch = [pltpu.VMEM((tm, d), BF16), pltpu.VMEM((tm, d), F32)]
    if sample:
        batch = state.shape[2]
        in_specs.append(pl.BlockSpec((None, width - 1, batch, tf), lambda j: (layer, 0, 0, j)))
        args.append(state)
        out_specs = [pl.BlockSpec((tm, d), row), pl.BlockSpec((width - 1, batch, tf), lambda j: (0, 0, j))]
        out_shape = [jax.ShapeDtypeStruct((m, d), F32), jax.ShapeDtypeStruct((width - 1, batch, hidden), F32)]
        body = functools.partial(_gated_sample_kernel, n_in=n_in, mode=mode, n_steps=n_steps)
        sem = ("arbitrary",)
    else:
        assert seq_len % tm == 0 and width - 1 <= SUBLANES <= tm
        out_specs = [pl.BlockSpec((tm, d), row), pl.BlockSpec((None, SUBLANES, tf), lambda i, j: (i, 0, j))]
        out_shape = [jax.ShapeDtypeStruct((m, d), F32), jax.ShapeDtypeStruct((m // tm, SUBLANES, hidden), F32)]
        scratch += [pltpu.VMEM((SUBLANES + tm, tf), F32), pltpu.VMEM((nj, SUBLANES, tf), F32)]
        body = functools.partial(_gated_prompt_kernel, n_in=n_in, mode=mode, tiles_per_seq=seq_len // tm)
        sem = ("arbitrary", "arbitrary")
    return pl.pallas_call(
        body, grid=grid, in_specs=in_specs, out_specs=out_specs, out_shape=out_shape,
        scratch_shapes=scratch, compiler_params=_params(*sem),
        name=f"gated_{mode}_{'sample' if sample else 'prompt'}",
    )(*args)


def _dwconv_prompt_kernel(z_ref, hist_ref, w_ref, y_ref, ext_ref, *, tiles_per_seq, row_chunk):
    tm = z_ref.shape[0]
    hist = hist_ref.shape[0]
    first = (pl.program_id(0) % tiles_per_seq) == 0

    @pl.when(first)
    def _():
        ext_ref[0:hist, :] = jnp.zeros(hist_ref.shape, F32)

    @pl.when(jnp.logical_not(first))
    def _():
        ext_ref[0:hist, :] = hist_ref[...]

    ext_ref[hist:hist + tm, :] = z_ref[...]
    y_ref[...] = _conv_rows(ext_ref, w_ref, tm, hist, row_chunk)


def _dwconv_prompt(z, w, layer, seq_len):
    m, d = z.shape
    width = w.shape[1]
    tm, tc = min(ROW_TILE, seq_len), min(CONV_COL_TILE, d)
    hist = _round_up(width - 1, SUBLANES)
    assert seq_len % tm == 0 and tm % hist == 0
    blocks_per_tile = tm // hist
    return pl.pallas_call(
        functools.partial(_dwconv_prompt_kernel, tiles_per_seq=seq_len // tm, row_chunk=min(64, tm)),
        grid=(m // tm, d // tc),
        in_specs=[
            pl.BlockSpec((tm, tc), lambda i, c: (i, c)),
            pl.BlockSpec((hist, tc), lambda i, c: (jnp.maximum(i * blocks_per_tile - 1, 0), c)),
            pl.BlockSpec((None, width, tc), lambda i, c: (layer, 0, c)),
        ],
        out_specs=pl.BlockSpec((tm, tc), lambda i, c: (i, c)),
        out_shape=jax.ShapeDtypeStruct((m, d), F32),
        scratch_shapes=[pltpu.VMEM((hist + tm, tc), F32)],
        compiler_params=_params("parallel", "parallel"),
        name="dwconv_prompt",
    )(z, z, w)


def _dwconv_sample_kernel(z_ref, st_ref, w_ref, y_ref, ns_ref, *, n_steps):
    n_hist = st_ref.shape[0]
    rb = z_ref.shape[0] // n_steps
    slabs = [st_ref[k] for k in range(n_hist)] + [z_ref[t * rb:(t + 1) * rb, :] for t in range(n_steps)]
    for k in range(n_hist):
        ns_ref[k] = slabs[n_steps + k]
    y = _conv_slabs(slabs, w_ref, n_steps)
    for t in range(n_steps):
        y_ref[t * rb:(t + 1) * rb, :] = y[t]


def _dwconv_sample(z, state, w, layer, n_steps):
    m, d = z.shape
    width = w.shape[1]
    batch = state.shape[2]
    tc = min(CONV_COL_TILE, d)
    return pl.pallas_call(
        functools.partial(_dwconv_sample_kernel, n_steps=n_steps),
        grid=(d // tc,),
        in_specs=[
            pl.BlockSpec((m, tc), lambda c: (0, c)),
            pl.BlockSpec((None, width - 1, batch, tc), lambda c: (layer, 0, 0, c)),
            pl.BlockSpec((None, width, tc), lambda c: (layer, 0, c)),
        ],
        out_specs=[pl.BlockSpec((m, tc), lambda c: (0, c)), pl.BlockSpec((width - 1, batch, tc), lambda c: (0, 0, c))],
        out_shape=[jax.ShapeDtypeStruct((m, d), F32), jax.ShapeDtypeStruct((width - 1, batch, d), F32)],
        compiler_params=_params("parallel"),
        name="dwconv_sample",
    )(z, state, w)


def _out_proj_kernel(h_ref, w_ref, b_ref, *rest, kind, chunk, n_groups, n_steps):
    j = pl.program_id(1)
    if kind == "conf":
        y_ref, bdw_ref, lg_ref, lb_ref, o_ref, s_ref = rest
    elif kind == "gmlp_prompt":
        u_ref, v_ref, lg_ref, lb_ref, ws_ref, bs_ref, o_ref, vlast_ref, s_ref = rest
    else:
        u_ref, v_ref, lg_ref, lb_ref, coef_ref, bs_ref, o_ref, vln_ref, s_ref = rest

    @pl.when(j == 0)
    def _():
        if kind == "conf":
            s_ref[...] = jax.nn.silu(_layernorm(y_ref[...] + bdw_ref[...], lg_ref[...], lb_ref[...])).astype(BF16)
        elif kind == "gmlp_prompt":
            tm, d = v_ref.shape
            gd = d // n_groups
            vln = _layernorm(v_ref[...], lg_ref[...], lb_ref[...])
            vlast_ref[...] = vln[tm - chunk:, :]
            vb = vln.astype(BF16)
            rows = lax.broadcasted_iota(jnp.int32, (chunk, chunk), 0)
            cols = lax.broadcasted_iota(jnp.int32, (chunk, chunk), 1)
            tril = (rows >= cols).astype(F32)
            for g in range(n_groups):
                wg = (ws_ref[g] * tril).astype(BF16)
                for c in range(tm // chunk):
                    rs, cs = slice(c * chunk, (c + 1) * chunk), slice(g * gd, (g + 1) * gd)
                    z = _dot(wg, vb[rs, cs]) + bs_ref[:, cs]
                    s_ref[rs, cs] = (u_ref[rs, cs] * z).astype(BF16)
        else:
            rb = v_ref.shape[0] // n_steps
            vln = _layernorm(v_ref[...], lg_ref[...], lb_ref[...])
            vln_ref[...] = vln
            vb = vln.astype(BF16).astype(F32)
            for t in range(n_steps):
                z = bs_ref[t:t + 1, :]
                for s in range(t + 1):
                    coef = coef_ref[t * n_steps + s:t * n_steps + s + 1, :].astype(BF16).astype(F32)
                    z = z + coef * vb[s * rb:(s + 1) * rb, :]
                s_ref[t * rb:(t + 1) * rb, :] = (u_ref[t * rb:(t + 1) * rb, :] * z).astype(BF16)

    o_ref[...] = h_ref[...] + _dot(s_ref[...], w_ref[...]) + b_ref[...]


def _out_proj(h, w, b, layer, kind, ins, in_specs_extra, extra_out=None, *, seq_len=None, chunk=0, n_groups=0, n_steps=0):
    m, d = h.shape
    tm = m if seq_len is None else min(ROW_TILE, seq_len)
    tn = min(COL_TILE, d)
    in_specs = [
        pl.BlockSpec((tm, tn), lambda i, j: (i, j)),
        pl.BlockSpec((None, d, tn), lambda i, j: (layer, 0, j)),
        pl.BlockSpec((None, 1, tn), lambda i, j: (layer, 0, j)),
    ] + in_specs_extra
    out_specs = [pl.BlockSpec((tm, tn), lambda i, j: (i, j))]
    out_shape = [jax.ShapeDtypeStruct((m, d), F32)]
    if extra_out is not None:
        out_specs.append(extra_out[0])
        out_shape.append(extra_out[1])
    return pl.pallas_call(
        functools.partial(_out_proj_kernel, kind=kind, chunk=chunk, n_groups=n_groups, n_steps=n_steps),
        grid=(m // tm, d // tn), in_specs=in_specs, out_specs=out_specs, out_shape=out_shape,
        scratch_shapes=[pltpu.VMEM((tm, d), BF16)],
        compiler_params=_params("arbitrary", "arbitrary"),
        name=f"out_proj_{kind}",
    )(h, w, b, *ins)


def _ple_kernel(h_ref, g_ref, wg_ref, p_ref, wp_ref, *rest, final, tn):
    if final:
        gf_ref, o_ref, y_ref = rest
    else:
        (o_ref,) = rest
    d = h_ref.shape[1]
    xn = _rmsnorm(h_ref[...], g_ref[...]).astype(BF16)
    pb = p_ref[...].astype(BF16)
    for c0 in range(0, d, tn):
        gate = jax.nn.sigmoid(_dot(xn, wg_ref[:, c0:c0 + tn]))
        o_ref[:, c0:c0 + tn] = h_ref[:, c0:c0 + tn] + gate * _dot(pb, wp_ref[:, c0:c0 + tn])
    if final:
        y_ref[...] = _rmsnorm(o_ref[...], gf_ref[...])


def _ple(h, g, wg, p, wp, layer, g_final=None):
    m, d = h.shape
    pdim = p.shape[2]
    tm = min(ROW_TILE, m)
    final = g_final is not None
    in_specs = [
        pl.BlockSpec((tm, d), lambda i: (i, 0)),
        pl.BlockSpec((None, 1, d), lambda i: (layer, 0, 0)),
        pl.BlockSpec((None, d, d), lambda i: (layer, 0, 0)),
        pl.BlockSpec((None, tm, pdim), lambda i: (layer, i, 0)),
        pl.BlockSpec((None, pdim, d), lambda i: (layer, 0, 0)),
    ]
    args = [h, g, wg, p, wp]
    out_specs = [pl.BlockSpec((tm, d), lambda i: (i, 0))]
    out_shape = [jax.ShapeDtypeStruct((m, d), F32)]
    if final:
        in_specs.append(pl.BlockSpec((1, d), lambda i: (0, 0)))
        args.append(g_final)
        out_specs.append(pl.BlockSpec((tm, d), lambda i: (i, 0)))
        out_shape.append(jax.ShapeDtypeStruct((m, d), F32))
    return pl.pallas_call(
        functools.partial(_ple_kernel, final=final, tn=min(COL_TILE, d)),
        grid=(m // tm,), in_specs=in_specs, out_specs=out_specs, out_shape=out_shape,
        compiler_params=_params("parallel"),
        name="ple_final" if final else "ple",
    )(*args)


def _trunk(h, p, wts, *, seq_len, n_steps, states):
    sample = states is not None
    depth = wts["g_mix"].shape[0]
    new_a, new_b, new_c, new_f = [], [], [], []
    y = None
    for i in range(depth):
        kind, l = i % 3, i // 3
        if kind == 0:
            (z,) = _in_proj(h, wts["g_mix"], i, wts["a_w_in"], wts["a_b_in"], l, "glu")
            if sample:
                conv, ns = _dwconv_sample(z, states[0], wts["a_w_dw"], l, n_steps)
            else:
                conv, ns = _dwconv_prompt(z, wts["a_w_dw"], l, seq_len), z
            new_a.append(ns)
            m, d = h.shape
            tm = m if sample else min(ROW_TILE, seq_len)
            vec = lambda i_, j_: (l, 0, 0)
            (h,) = _out_proj(
                h, wts["a_w_out"], wts["a_b_out"], l, "conf",
                [conv, wts["a_b_dw"], wts["a_ln_g"], wts["a_ln_b"]],
                [pl.BlockSpec((tm, d), lambda i_, j_: (i_, 0))] + [pl.BlockSpec((None, 1, d), vec)] * 3,
                seq_len=seq_len)
        elif kind == 1:
            h, ns = _gated(h, wts["g_mix"], i, [(wts["b_w_in"], 0), (wts["b_w_in"], 1), (wts["b_w_in"], 2)],
                           wts["b_w_dw"], wts["b_zero_bias"], wts["b_w_out"], l, "short",
                           seq_len=seq_len, state=states[1] if sample else None, n_steps=n_steps)
            new_b.append(ns)
        else:
            u, v = _in_proj(h, wts["g_mix"], i, wts["c_w_in"], wts["c_b_in"], l, "gelu2")
            m, d = h.shape
            chunk = wts["c_w_s"].shape[2]
            n_groups = wts["c_w_s"].shape[1]
            vec = lambda i_, j_: (l, 0, 0)
            if sample:
                full = pl.BlockSpec((m, d), lambda i_, j_: (0, 0))
                h, nv = _out_proj(
                    h, wts["c_w_out"], wts["c_b_out"], l, "gmlp_sample",
                    [u, v, wts["c_ln_g"], wts["c_ln_b"], wts["c_coef"], wts["c_bs_steps"]],
                    [full, full] + [pl.BlockSpec((None, 1, d), vec)] * 2
                    + [pl.BlockSpec((None, n_steps * n_steps, d), vec), pl.BlockSpec((None, n_steps, d), vec)],
                    (full, jax.ShapeDtypeStruct((m, d), F32)), n_steps=n_steps)
            else:
                tm = min(ROW_TILE, seq_len)
                assert seq_len % chunk == 0 and tm % chunk == 0
                tps = seq_len // tm
                rows = pl.BlockSpec((tm, d), lambda i_, j_: (i_, 0))
                h, nv = _out_proj(
                    h, wts["c_w_out"], wts["c_b_out"], l, "gmlp_prompt",
                    [u, v, wts["c_ln_g"], wts["c_ln_b"], wts["c_w_s"], wts["c_bs_rows"]],
                    [rows, rows] + [pl.BlockSpec((None, 1, d), vec)] * 2
                    + [pl.BlockSpec((None, n_groups, chunk, chunk), lambda i_, j_: (l, 0, 0, 0)),
                       pl.BlockSpec((None, chunk, d), vec)],
                    (pl.BlockSpec((None, chunk, d), lambda i_, j_: (i_ // tps, 0, 0)),
                     jax.ShapeDtypeStruct((m // seq_len, chunk, d), F32)),
                    seq_len=seq_len, chunk=chunk, n_groups=n_groups)
            new_c.append(nv)
        h, nf = _gated(h, wts["g_ffn"], i, [(wts["f_w_gate"], 0), (wts["f_w_up"], 0)],
                       wts["f_w_dw"], wts["f_b_dw"], wts["f_w_down"], i, "ffn",
                       seq_len=seq_len, state=states[2] if sample else None, n_steps=n_steps)
        new_f.append(nf)
        if i == depth - 1:
            h, y = _ple(h, wts["g_ple"], wts["ple_w_gate"], p, wts["ple_w_proj"], i, wts["g_final"])
        else:
            (h,) = _ple(h, wts["g_ple"], wts["ple_w_gate"], p, wts["ple_w_proj"], i)
    return y, new_a, new_b, new_c, new_f


def kernel(x_prompt, x_sample, p_prompt, p_sample, state_conformer, state_shortconv, state_ffn, g_mix, g_ffn, g_ple, g_final, a_w_in, a_b_in, a_w_dw, a_b_dw, a_ln_g, a_ln_b, a_w_out, a_b_out, b_w_in, b_w_dw, b_w_out, c_w_in, c_b_in, c_ln_g, c_ln_b, c_w_s, c_b_s, c_w_out, c_b_out, f_w_gate, f_w_dw, f_b_dw, f_w_up, f_w_down, ple_w_proj, ple_w_gate):
    bp, seq_len, d = x_prompt.shape
    bs, n_steps, _ = x_sample.shape
    depth = g_mix.shape[0]
    n_groups, chunk = c_w_s.shape[1], c_w_s.shape[2]
    gd = d // n_groups
    assert n_steps <= chunk

    row3 = lambda a: a[:, None, :]
    wts = dict(
        g_mix=row3(g_mix), g_ffn=row3(g_ffn), g_ple=row3(g_ple), g_final=g_final[None, :],
        a_w_in=a_w_in.astype(BF16), a_b_in=row3(a_b_in), a_w_dw=a_w_dw, a_b_dw=row3(a_b_dw),
        a_ln_g=row3(a_ln_g), a_ln_b=row3(a_ln_b), a_w_out=a_w_out.astype(BF16), a_b_out=row3(a_b_out),
        b_w_in=b_w_in.astype(BF16), b_w_dw=b_w_dw, b_w_out=b_w_out.astype(BF16),
        b_zero_bias=jnp.zeros((b_w_dw.shape[0], 1, d), F32),
        c_w_in=c_w_in.astype(BF16), c_b_in=row3(c_b_in), c_ln_g=row3(c_ln_g), c_ln_b=row3(c_ln_b),
        c_w_s=c_w_s, c_w_out=c_w_out.astype(BF16), c_b_out=row3(c_b_out),
        c_bs_rows=jnp.repeat(jnp.swapaxes(c_b_s, 1, 2), gd, axis=2),
        c_bs_steps=jnp.repeat(jnp.swapaxes(c_b_s, 1, 2)[:, :n_steps], gd, axis=2),
        c_coef=jnp.repeat(jnp.transpose(c_w_s[:, :, :n_steps, :n_steps], (0, 2, 3, 1)).reshape(
            c_w_s.shape[0], n_steps * n_steps, n_groups), gd, axis=2),
        f_w_gate=f_w_gate.astype(BF16), f_w_dw=f_w_dw, f_b_dw=row3(f_b_dw),
        f_w_up=f_w_up.astype(BF16), f_w_down=f_w_down.astype(BF16),
        ple_w_proj=ple_w_proj.astype(BF16), ple_w_gate=ple_w_gate.astype(BF16),
    )

    hp = x_prompt.reshape(bp * seq_len, d)
    pp = p_prompt.reshape(depth, bp * seq_len, -1)
    yp, a_p, b_p, c_p, f_p = _trunk(hp, pp, wts, seq_len=seq_len, n_steps=None, states=None)

    tmaj = lambda a: jnp.swapaxes(a, 1, 2)
    hs = jnp.swapaxes(x_sample, 0, 1).reshape(n_steps * bs, d)
    ps = tmaj(p_sample).reshape(depth, n_steps * bs, -1)
    states = (tmaj(state_conformer), tmaj(state_shortconv), tmaj(state_ffn))
    ys, a_s, b_s, c_s, f_s = _trunk(hs, ps, wts, seq_len=None, n_steps=n_steps, states=states)

    tm = min(ROW_TILE, seq_len)
    tps = seq_len // tm

    def prompt_tail(tails, width):
        return tails[tps - 1::tps, SUBLANES - (width - 1):, :]

    bmaj = lambda a: jnp.swapaxes(a, 0, 1)
    wa, wb, wf = a_w_dw.shape[1], b_w_dw.shape[1], f_w_dw.shape[1]
    return (
        yp.reshape(bp, seq_len, d),
        bmaj(ys.reshape(n_steps, bs, d)),
        jnp.stack([z.reshape(bp, seq_len, d)[:, seq_len - (wa - 1):] for z in a_p]),
        jnp.stack([bmaj(s) for s in a_s]),
        jnp.stack([prompt_tail(t, wb) for t in b_p]),
        jnp.stack([bmaj(s) for s in b_s]),
        jnp.stack(c_p),
        jnp.stack([bmaj(v.reshape(n_steps, bs, d)) for v in c_s]),
        jnp.stack([prompt_tail(t, wf) for t in f_p]),
        jnp.stack([bmaj(s) for s in f_s]),
    )
```

```python
import functools

import jax
import jax.numpy as jnp
from jax import lax
from jax.experimental import pallas as pl
from jax.experimental.pallas import tpu as pltpu

EPS = 1e-6
F32 = jnp.float32
BF16 = jnp.bfloat16

ROW_TILE = 512
PROJ_ROW_TILE = 1024
COL_TILE = 512
CONV_COL_TILE = 256
CONV_ROW_CHUNK = 64
GATE_ROW_CHUNK = 64
SUBLANES = 8
VMEM_LIMIT = 56 * 1024 * 1024


def _params(*sem):
    return pltpu.CompilerParams(dimension_semantics=sem, vmem_limit_bytes=VMEM_LIMIT)


def _rmsnorm(x, g):
    return x * lax.rsqrt(jnp.mean(x * x, axis=-1, keepdims=True) + EPS) * g


def _layernorm(x, g, b):
    xc = x - jnp.mean(x, axis=-1, keepdims=True)
    return xc * lax.rsqrt(jnp.mean(xc * xc, axis=-1, keepdims=True) + EPS) * g + b


def _dot(a, b):
    return jnp.dot(a, b, preferred_element_type=F32)


def _round_up(n, m):
    return -(-n // m) * m


def _conv_slabs(slabs, w_ref, n_out):
    width = w_ref.shape[0]
    out = []
    for t in range(n_out):
        acc = None
        for k in range(width):
            term = w_ref[k:k + 1, :] * slabs[t + k]
            acc = term if acc is None else acc + term
        out.append(acc)
    return out


def _in_proj_kernel(h_ref, g_ref, w0_ref, w1_ref, b0_ref, b1_ref, *rest, kind):
    outs, xn_ref = rest[:-1], rest[-1]

    @pl.when(pl.program_id(1) == 0)
    def _():
        xn_ref[...] = _rmsnorm(h_ref[...], g_ref[...]).astype(BF16)

    xn = xn_ref[...]
    a = _dot(xn, w0_ref[...]) + b0_ref[...]
    b = _dot(xn, w1_ref[...]) + b1_ref[...]
    if kind == "glu":
        outs[0][...] = a * jax.nn.sigmoid(b)
    else:
        outs[0][...] = jax.nn.gelu(a)
        outs[1][...] = jax.nn.gelu(b)


def _in_proj(h, g, g_layer, w, b, layer, kind):
    m, d = h.shape
    tm, tn = min(PROJ_ROW_TILE, m), min(COL_TILE, d)
    nj = d // tn
    n_out = 1 if kind == "glu" else 2
    return pl.pallas_call(
        functools.partial(_in_proj_kernel, kind=kind),
        grid=(m // tm, nj),
        in_specs=[
            pl.BlockSpec((tm, d), lambda i, j: (i, 0)),
            pl.BlockSpec((None, 1, d), lambda i, j: (g_layer, 0, 0)),
            pl.BlockSpec((None, d, tn), lambda i, j: (layer, 0, j)),
            pl.BlockSpec((None, d, tn), lambda i, j: (layer, 0, j + nj)),
            pl.BlockSpec((None, 1, tn), lambda i, j: (layer, 0, j)),
            pl.BlockSpec((None, 1, tn), lambda i, j: (layer, 0, j + nj)),
        ],
        out_specs=[pl.BlockSpec((tm, tn), lambda i, j: (i, j))] * n_out,
        out_shape=[jax.ShapeDtypeStruct((m, d), F32)] * n_out,
        scratch_shapes=[pltpu.VMEM((tm, d), BF16)],
        compiler_params=_params("parallel", "arbitrary"),
        name=f"in_proj_{kind}",
    )(h, g, w, w, b, b)


def _gate(conv, bias, other, mode):
    if mode == "ffn":
        return jax.nn.silu(conv + bias) * other
    return other * conv


def _gated_prompt_kernel(*refs, n_in, mode, tiles_per_seq, row_chunk):
    h_ref, g_ref = refs[:2]
    w_in = refs[2:2 + n_in]
    wdw_ref, bdw_ref, w2_ref, ho_ref, tail_ref, xn_ref, ext_ref, oth_ref, hid_ref, carry_ref = refs[2 + n_in:]
    i, j = pl.program_id(0), pl.program_id(1)
    tm, tf = oth_ref.shape
    hist = SUBLANES
    width = wdw_ref.shape[0]
    base = hist - (width - 1)

    @pl.when(j == 0)
    def _():
        xn_ref[...] = _rmsnorm(h_ref[...], g_ref[...]).astype(BF16)
        ho_ref[...] = h_ref[...]

        @pl.when(i % tiles_per_seq == 0)
        def _():
            carry_ref[...] = jnp.zeros(carry_ref.shape, F32)

    xn = xn_ref[...]
    ext_ref[0:hist, :] = carry_ref[j]
    if mode == "ffn":
        ext_ref[hist:hist + tm, :] = _dot(xn, w_in[0][...])
        oth_ref[...] = _dot(xn, w_in[1][...])
    else:
        oth_ref[...] = _dot(xn, w_in[0][...])
        ext_ref[hist:hist + tm, :] = _dot(xn, w_in[1][...]) * _dot(xn, w_in[2][...])
    tail = ext_ref[tm:tm + hist, :]
    carry_ref[j] = tail
    tail_ref[...] = tail

    for r0 in range(0, tm, row_chunk):
        for c0 in range(0, tf, 128):
            cs = slice(c0, c0 + 128)
            conv = None
            for k in range(width):
                term = wdw_ref[k:k + 1, cs] * ext_ref[base + k + r0:base + k + r0 + row_chunk, cs]
                conv = term if conv is None else conv + term
            hid = _gate(conv, bdw_ref[:, cs], oth_ref[r0:r0 + row_chunk, cs], mode)
            hid_ref[r0:r0 + row_chunk, cs] = hid.astype(BF16)

    ho_ref[...] += _dot(hid_ref[...], w2_ref[...])


def _gated_sample_kernel(*refs, n_in, mode, n_steps):
    h_ref, g_ref = refs[:2]
    w_in = refs[2:2 + n_in]
    wdw_ref, bdw_ref, w2_ref, st_ref, ho_ref, ns_ref, xn_ref = refs[2 + n_in:]
    j = pl.program_id(0)
    n_hist = st_ref.shape[0]
    rb = h_ref.shape[0] // n_steps

    @pl.when(j == 0)
    def _():
        xn_ref[...] = _rmsnorm(h_ref[...], g_ref[...]).astype(BF16)
        ho_ref[...] = h_ref[...]

    xn = xn_ref[...]
    proj = [_dot(xn, w[...]) for w in w_in]
    v, other = (proj[0], proj[1]) if mode == "ffn" else (proj[1] * proj[2], proj[0])
    slabs = [st_ref[k] for k in range(n_hist)] + [v[t * rb:(t + 1) * rb, :] for t in range(n_steps)]
    for k in range(n_hist):
        ns_ref[k] = slabs[n_steps + k]
    conv = jnp.concatenate(_conv_slabs(slabs, wdw_ref, n_steps), axis=0)
    hid = _gate(conv, bdw_ref[...], other, mode)
    ho_ref[...] += _dot(hid.astype(BF16), w2_ref[...])


def _gated(h, g, g_layer, w_in_list, wdw, bdw, w2, layer, mode, *, seq_len=None, state=None, n_steps=None):
    m, d = h.shape
    hidden = w2.shape[1]
    tf = min(COL_TILE, hidden)
    nj = hidden // tf
    width = wdw.shape[1]
    n_in = len(w_in_list)
    sample = state is not None
    tm = m if sample else min(ROW_TILE, seq_len)
    if sample:
        row = lambda j: (0, 0)
        col = lambda off: (lambda j: (layer, 0, j + off))
        vec = lambda l: (lambda j: (l, 0, 0))
        hid_vec = lambda j: (layer, 0, j)
        w2_map = lambda j: (layer, j, 0)
        grid = (nj,)
    else:
        row = lambda i, j: (i, 0)
        col = lambda off: (lambda i, j: (layer, 0, j + off))
        vec = lambda l: (lambda i, j: (l, 0, 0))
        hid_vec = lambda i, j: (layer, 0, j)
        w2_map = lambda i, j: (layer, j, 0)
        grid = (m // tm, nj)
    in_specs = [pl.BlockSpec((tm, d), row), pl.BlockSpec((None, 1, d), vec(g_layer))]
    in_specs += [pl.BlockSpec((None, d, tf), col(off * nj)) for _, off in w_in_list]
    in_specs += [
        pl.BlockSpec((None, width, tf), hid_vec),
        pl.BlockSpec((None, 1, tf), hid_vec),
        pl.BlockSpec((None, tf, d), w2_map),
    ]
    args = [h, g] + [w for w, _ in w_in_list] + [wdw, bdw, w2]
    scratch = [pltpu.VMEM((tm, d), BF16)]
    if sample:
        batch = state.shape[2]
        in_specs.append(pl.BlockSpec((None, width - 1, batch, tf), lambda j: (layer, 0, 0, j)))
        args.append(state)
        out_specs = [pl.BlockSpec((tm, d), row), pl.BlockSpec((width - 1, batch, tf), lambda j: (0, 0, j))]
        out_shape = [jax.ShapeDtypeStruct((m, d), F32), jax.ShapeDtypeStruct((width - 1, batch, hidden), F32)]
        body = functools.partial(_gated_sample_kernel, n_in=n_in, mode=mode, n_steps=n_steps)
        sem = ("arbitrary",)
    else:
        assert seq_len % tm == 0 and width - 1 <= SUBLANES <= tm
        out_specs = [pl.BlockSpec((tm, d), row), pl.BlockSpec((None, SUBLANES, tf), lambda i, j: (i, 0, j))]
        out_shape = [jax.ShapeDtypeStruct((m, d), F32), jax.ShapeDtypeStruct((m // tm, SUBLANES, hidden), F32)]
        scratch += [pltpu.VMEM((SUBLANES + tm, tf), F32), pltpu.VMEM((tm, tf), F32), pltpu.VMEM((tm, tf), BF16),
                    pltpu.VMEM((nj, SUBLANES, tf), F32)]
        body = functools.partial(_gated_prompt_kernel, n_in=n_in, mode=mode, tiles_per_seq=seq_len // tm,
                                 row_chunk=min(GATE_ROW_CHUNK, tm))
        sem = ("arbitrary", "arbitrary")
    return pl.pallas_call(
        body, grid=grid, in_specs=in_specs, out_specs=out_specs, out_shape=out_shape,
        scratch_shapes=scratch, compiler_params=_params(*sem),
        name=f"gated_{mode}_{'sample' if sample else 'prompt'}",
    )(*args)


def _dwconv_prompt_kernel(z_ref, hist_ref, w_ref, y_ref, ext_ref, *, tiles_per_seq, row_chunk):
    tm, tc = z_ref.shape
    hist = hist_ref.shape[0]
    width = w_ref.shape[0]
    base = hist - (width - 1)
    first = (pl.program_id(0) % tiles_per_seq) == 0

    @pl.when(first)
    def _():
        ext_ref[0, 0:hist, :] = jnp.zeros(hist_ref.shape, F32)

    @pl.when(jnp.logical_not(first))
    def _():
        ext_ref[0, 0:hist, :] = hist_ref[...]

    ext_ref[0, hist:hist + tm, :] = z_ref[...]
    n_shifted = hist + tm - SUBLANES
    for s in range(1, SUBLANES):
        ext_ref[s, 0:n_shifted, :] = ext_ref[0, s:s + n_shifted, :]

    def block(r, carry):
        r0 = pl.multiple_of(r * row_chunk, row_chunk)
        for c0 in range(0, tc, 128):
            cs = slice(c0, c0 + 128)
            acc = None
            for k in range(width):
                q, s = divmod(base + k, SUBLANES)
                term = w_ref[k:k + 1, cs] * ext_ref[s, pl.ds(r0 + q * SUBLANES, row_chunk), cs]
                acc = term if acc is None else acc + term
            y_ref[pl.ds(r0, row_chunk), cs] = acc
        return carry

    lax.fori_loop(0, tm // row_chunk, block, 0)


def _dwconv_prompt(z, w, layer, seq_len):
    m, d = z.shape
    width = w.shape[1]
    tm, tc = min(ROW_TILE, seq_len), min(CONV_COL_TILE, d)
    hist = _round_up(width - 1, SUBLANES)
    assert seq_len % tm == 0 and tm % hist == 0
    blocks_per_tile = tm // hist
    return pl.pallas_call(
        functools.partial(_dwconv_prompt_kernel, tiles_per_seq=seq_len // tm, row_chunk=min(CONV_ROW_CHUNK, tm)),
        grid=(m // tm, d // tc),
        in_specs=[
            pl.BlockSpec((tm, tc), lambda i, c: (i, c)),
            pl.BlockSpec((hist, tc), lambda i, c: (jnp.maximum(i * blocks_per_tile - 1, 0), c)),
            pl.BlockSpec((None, width, tc), lambda i, c: (layer, 0, c)),
        ],
        out_specs=pl.BlockSpec((tm, tc), lambda i, c: (i, c)),
        out_shape=jax.ShapeDtypeStruct((m, d), F32),
        scratch_shapes=[pltpu.VMEM((SUBLANES, hist + tm, tc), F32)],
        compiler_params=_params("parallel", "parallel"),
        name="dwconv_prompt",
    )(z, z, w)


def _dwconv_sample_kernel(z_ref, st_ref, w_ref, y_ref, ns_ref, *, n_steps):
    n_hist = st_ref.shape[0]
    rb = z_ref.shape[0] // n_steps
    slabs = [st_ref[k] for k in range(n_hist)] + [z_ref[t * rb:(t + 1) * rb, :] for t in range(n_steps)]
    for k in range(n_hist):
        ns_ref[k] = slabs[n_steps + k]
    y = _conv_slabs(slabs, w_ref, n_steps)
    for t in range(n_steps):
        y_ref[t * rb:(t + 1) * rb, :] = y[t]


def _dwconv_sample(z, state, w, layer, n_steps):
    m, d = z.shape
    width = w.shape[1]
    batch = state.shape[2]
    tc = min(CONV_COL_TILE, d)
    return pl.pallas_call(
        functools.partial(_dwconv_sample_kernel, n_steps=n_steps),
        grid=(d // tc,),
        in_specs=[
            pl.BlockSpec((m, tc), lambda c: (0, c)),
            pl.BlockSpec((None, width - 1, batch, tc), lambda c: (layer, 0, 0, c)),
            pl.BlockSpec((None, width, tc), lambda c: (layer, 0, c)),
        ],
        out_specs=[pl.BlockSpec((m, tc), lambda c: (0, c)), pl.BlockSpec((width - 1, batch, tc), lambda c: (0, 0, c))],
        out_shape=[jax.ShapeDtypeStruct((m, d), F32), jax.ShapeDtypeStruct((width - 1, batch, d), F32)],
        compiler_params=_params("parallel"),
        name="dwconv_sample",
    )(z, state, w)


def _out_proj_kernel(h_ref, w_ref, b_ref, *rest, kind, chunk, n_groups, n_steps):
    j = pl.program_id(1)
    if kind == "conf":
        y_ref, bdw_ref, lg_ref, lb_ref, o_ref, s_ref = rest
    elif kind == "gmlp_prompt":
        u_ref, v_ref, lg_ref, lb_ref, ws_ref, bs_ref, o_ref, vlast_ref, s_ref = rest
    else:
        u_ref, v_ref, lg_ref, lb_ref, coef_ref, bs_ref, o_ref, vln_ref, s_ref = rest

    @pl.when(j == 0)
    def _():
        if kind == "conf":
            s_ref[...] = jax.nn.silu(_layernorm(y_ref[...] + bdw_ref[...], lg_ref[...], lb_ref[...])).astype(BF16)
        elif kind == "gmlp_prompt":
            tm, d = v_ref.shape
            gd = d // n_groups
            vln = _layernorm(v_ref[...], lg_ref[...], lb_ref[...])
            vlast_ref[...] = vln[tm - chunk:, :]
            vb = vln.astype(BF16)
            rows = lax.broadcasted_iota(jnp.int32, (chunk, chunk), 0)
            cols = lax.broadcasted_iota(jnp.int32, (chunk, chunk), 1)
            tril = (rows >= cols).astype(F32)
            for g in range(n_groups):
                wg = (ws_ref[g] * tril).astype(BF16)
                for c in range(tm // chunk):
                    rs, cs = slice(c * chunk, (c + 1) * chunk), slice(g * gd, (g + 1) * gd)
                    z = _dot(wg, vb[rs, cs]) + bs_ref[:, cs]
                    s_ref[rs, cs] = (u_ref[rs, cs] * z).astype(BF16)
        else:
            rb = v_ref.shape[0] // n_steps
            vln = _layernorm(v_ref[...], lg_ref[...], lb_ref[...])
            vln_ref[...] = vln
            vb = vln.astype(BF16).astype(F32)
            for t in range(n_steps):
                z = bs_ref[t:t + 1, :]
                for s in range(t + 1):
                    coef = coef_ref[t * n_steps + s:t * n_steps + s + 1, :].astype(BF16).astype(F32)
                    z = z + coef * vb[s * rb:(s + 1) * rb, :]
                s_ref[t * rb:(t + 1) * rb, :] = (u_ref[t * rb:(t + 1) * rb, :] * z).astype(BF16)

    o_ref[...] = h_ref[...] + _dot(s_ref[...], w_ref[...]) + b_ref[...]


def _out_proj(h, w, b, layer, kind, tm, ins, in_specs_extra, extra_out=None, *, chunk=0, n_groups=0, n_steps=0):
    m, d = h.shape
    tn = min(COL_TILE, d)
    in_specs = [
        pl.BlockSpec((tm, tn), lambda i, j: (i, j)),
        pl.BlockSpec((None, d, tn), lambda i, j: (layer, 0, j)),
        pl.BlockSpec((None, 1, tn), lambda i, j: (layer, 0, j)),
    ] + in_specs_extra
    out_specs = [pl.BlockSpec((tm, tn), lambda i, j: (i, j))]
    out_shape = [jax.ShapeDtypeStruct((m, d), F32)]
    if extra_out is not None:
        out_specs.append(extra_out[0])
        out_shape.append(extra_out[1])
    return pl.pallas_call(
        functools.partial(_out_proj_kernel, kind=kind, chunk=chunk, n_groups=n_groups, n_steps=n_steps),
        grid=(m // tm, d // tn), in_specs=in_specs, out_specs=out_specs, out_shape=out_shape,
        scratch_shapes=[pltpu.VMEM((tm, d), BF16)],
        compiler_params=_params("arbitrary", "arbitrary"),
        name=f"out_proj_{kind}",
    )(h, w, b, *ins)


def _ple_kernel(h_ref, g_ref, wg_ref, p_ref, wp_ref, *rest, final, tn):
    if final:
        gf_ref, o_ref, y_ref = rest
    else:
        (o_ref,) = rest
    d = h_ref.shape[1]
    xn = _rmsnorm(h_ref[...], g_ref[...]).astype(BF16)
    pb = p_ref[...].astype(BF16)
    for c0 in range(0, d, tn):
        gate = jax.nn.sigmoid(_dot(xn, wg_ref[:, c0:c0 + tn]))
        o_ref[:, c0:c0 + tn] = h_ref[:, c0:c0 + tn] + gate * _dot(pb, wp_ref[:, c0:c0 + tn])
    if final:
        y_ref[...] = _rmsnorm(o_ref[...], gf_ref[...])


def _ple(h, g, wg, p, wp, layer, g_final=None):
    m, d = h.shape
    pdim = p.shape[2]
    tm = min(ROW_TILE, m)
    final = g_final is not None
    in_specs = [
        pl.BlockSpec((tm, d), lambda i: (i, 0)),
        pl.BlockSpec((None, 1, d), lambda i: (layer, 0, 0)),
        pl.BlockSpec((None, d, d), lambda i: (layer, 0, 0)),
        pl.BlockSpec((None, tm, pdim), lambda i: (layer, i, 0)),
        pl.BlockSpec((None, pdim, d), lambda i: (layer, 0, 0)),
    ]
    args = [h, g, wg, p, wp]
    out_specs = [pl.BlockSpec((tm, d), lambda i: (i, 0))]
    out_shape = [jax.ShapeDtypeStruct((m, d), F32)]
    if final:
        in_specs.append(pl.BlockSpec((1, d), lambda i: (0, 0)))
        args.append(g_final)
        out_specs.append(pl.BlockSpec((tm, d), lambda i: (i, 0)))
        out_shape.append(jax.ShapeDtypeStruct((m, d), F32))
    return pl.pallas_call(
        functools.partial(_ple_kernel, final=final, tn=min(COL_TILE, d)),
        grid=(m // tm,), in_specs=in_specs, out_specs=out_specs, out_shape=out_shape,
        compiler_params=_params("parallel"),
        name="ple_final" if final else "ple",
    )(*args)


def _trunk(h, p, wts, *, seq_len, n_steps, states):
    sample = states is not None
    depth = wts["g_mix"].shape[0]
    new_a, new_b, new_c, new_f = [], [], [], []
    y = None
    for i in range(depth):
        kind, l = i % 3, i // 3
        if kind == 0:
            (z,) = _in_proj(h, wts["g_mix"], i, wts["a_w_in"], wts["a_b_in"], l, "glu")
            if sample:
                conv, ns = _dwconv_sample(z, states[0], wts["a_w_dw"], l, n_steps)
            else:
                conv, ns = _dwconv_prompt(z, wts["a_w_dw"], l, seq_len), z
            new_a.append(ns)
            m, d = h.shape
            tm = min(PROJ_ROW_TILE, m)
            vec = lambda i_, j_: (l, 0, 0)
            (h,) = _out_proj(
                h, wts["a_w_out"], wts["a_b_out"], l, "conf", tm,
                [conv, wts["a_b_dw"], wts["a_ln_g"], wts["a_ln_b"]],
                [pl.BlockSpec((tm, d), lambda i_, j_: (i_, 0))] + [pl.BlockSpec((None, 1, d), vec)] * 3)
        elif kind == 1:
            h, ns = _gated(h, wts["g_mix"], i, [(wts["b_w_in"], 0), (wts["b_w_in"], 1), (wts["b_w_in"], 2)],
                           wts["b_w_dw"], wts["b_zero_bias"], wts["b_w_out"], l, "short",
                           seq_len=seq_len, state=states[1] if sample else None, n_steps=n_steps)
            new_b.append(ns)
        else:
            u, v = _in_proj(h, wts["g_mix"], i, wts["c_w_in"], wts["c_b_in"], l, "gelu2")
            m, d = h.shape
            chunk = wts["c_w_s"].shape[2]
            n_groups = wts["c_w_s"].shape[1]
            vec = lambda i_, j_: (l, 0, 0)
            if sample:
                full = pl.BlockSpec((m, d), lambda i_, j_: (0, 0))
                h, nv = _out_proj(
                    h, wts["c_w_out"], wts["c_b_out"], l, "gmlp_sample", m,
                    [u, v, wts["c_ln_g"], wts["c_ln_b"], wts["c_coef"], wts["c_bs_steps"]],
                    [full, full] + [pl.BlockSpec((None, 1, d), vec)] * 2
                    + [pl.BlockSpec((None, n_steps * n_steps, d), vec), pl.BlockSpec((None, n_steps, d), vec)],
                    (full, jax.ShapeDtypeStruct((m, d), F32)), n_steps=n_steps)
            else:
                tm = min(ROW_TILE, seq_len)
                assert seq_len % chunk == 0 and tm % chunk == 0
                tps = seq_len // tm
                rows = pl.BlockSpec((tm, d), lambda i_, j_: (i_, 0))
                h, nv = _out_proj(
                    h, wts["c_w_out"], wts["c_b_out"], l, "gmlp_prompt", tm,
                    [u, v, wts["c_ln_g"], wts["c_ln_b"], wts["c_w_s"], wts["c_bs_rows"]],
                    [rows, rows] + [pl.BlockSpec((None, 1, d), vec)] * 2
                    + [pl.BlockSpec((None, n_groups, chunk, chunk), lambda i_, j_: (l, 0, 0, 0)),
                       pl.BlockSpec((None, chunk, d), vec)],
                    (pl.BlockSpec((None, chunk, d), lambda i_, j_: (i_ // tps, 0, 0)),
                     jax.ShapeDtypeStruct((m // seq_len, chunk, d), F32)),
                    chunk=chunk, n_groups=n_groups)
            new_c.append(nv)
        h, nf = _gated(h, wts["g_ffn"], i, [(wts["f_w_gate"], 0), (wts["f_w_up"], 0)],
                       wts["f_w_dw"], wts["f_b_dw"], wts["f_w_down"], i, "ffn",
                       seq_len=seq_len, state=states[2] if sample else None, n_steps=n_steps)
        new_f.append(nf)
        if i == depth - 1:
            h, y = _ple(h, wts["g_ple"], wts["ple_w_gate"], p, wts["ple_w_proj"], i, wts["g_final"])
        else:
            (h,) = _ple(h, wts["g_ple"], wts["ple_w_gate"], p, wts["ple_w_proj"], i)
    return y, new_a, new_b, new_c, new_f


def kernel(x_prompt, x_sample, p_prompt, p_sample, state_conformer, state_shortconv, state_ffn, g_mix, g_ffn, g_ple, g_final, a_w_in, a_b_in, a_w_dw, a_b_dw, a_ln_g, a_ln_b, a_w_out, a_b_out, b_w_in, b_w_dw, b_w_out, c_w_in, c_b_in, c_ln_g, c_ln_b, c_w_s, c_b_s, c_w_out, c_b_out, f_w_gate, f_w_dw, f_b_dw, f_w_up, f_w_down, ple_w_proj, ple_w_gate):
    bp, seq_len, d = x_prompt.shape
    bs, n_steps, _ = x_sample.shape
    depth = g_mix.shape[0]
    n_groups, chunk = c_w_s.shape[1], c_w_s.shape[2]
    gd = d // n_groups
    assert n_steps <= chunk

    row3 = lambda a: a[:, None, :]
    wts = dict(
        g_mix=row3(g_mix), g_ffn=row3(g_ffn), g_ple=row3(g_ple), g_final=g_final[None, :],
        a_w_in=a_w_in.astype(BF16), a_b_in=row3(a_b_in), a_w_dw=a_w_dw, a_b_dw=row3(a_b_dw),
        a_ln_g=row3(a_ln_g), a_ln_b=row3(a_ln_b), a_w_out=a_w_out.astype(BF16), a_b_out=row3(a_b_out),
        b_w_in=b_w_in.astype(BF16), b_w_dw=b_w_dw, b_w_out=b_w_out.astype(BF16),
        b_zero_bias=jnp.zeros((b_w_dw.shape[0], 1, d), F32),
        c_w_in=c_w_in.astype(BF16), c_b_in=row3(c_b_in), c_ln_g=row3(c_ln_g), c_ln_b=row3(c_ln_b),
        c_w_s=c_w_s, c_w_out=c_w_out.astype(BF16), c_b_out=row3(c_b_out),
        c_bs_rows=jnp.repeat(jnp.swapaxes(c_b_s, 1, 2), gd, axis=2),
        c_bs_steps=jnp.repeat(jnp.swapaxes(c_b_s, 1, 2)[:, :n_steps], gd, axis=2),
        c_coef=jnp.repeat(jnp.transpose(c_w_s[:, :, :n_steps, :n_steps], (0, 2, 3, 1)).reshape(
            c_w_s.shape[0], n_steps * n_steps, n_groups), gd, axis=2),
        f_w_gate=f_w_gate.astype(BF16), f_w_dw=f_w_dw, f_b_dw=row3(f_b_dw),
        f_w_up=f_w_up.astype(BF16), f_w_down=f_w_down.astype(BF16),
        ple_w_proj=ple_w_proj.astype(BF16), ple_w_gate=ple_w_gate.astype(BF16),
    )

    hp = x_prompt.reshape(bp * seq_len, d)
    pp = p_prompt.reshape(depth, bp * seq_len, -1)
    yp, a_p, b_p, c_p, f_p = _trunk(hp, pp, wts, seq_len=seq_len, n_steps=None, states=None)

    tmaj = lambda a: jnp.swapaxes(a, 1, 2)
    hs = jnp.swapaxes(x_sample, 0, 1).reshape(n_steps * bs, d)
    ps = tmaj(p_sample).reshape(depth, n_steps * bs, -1)
    states = (tmaj(state_conformer), tmaj(state_shortconv), tmaj(state_ffn))
    ys, a_s, b_s, c_s, f_s = _trunk(hs, ps, wts, seq_len=None, n_steps=n_steps, states=states)

    tm = min(ROW_TILE, seq_len)
    tps = seq_len // tm

    def prompt_tail(tails, width):
        return tails[tps - 1::tps, SUBLANES - (width - 1):, :]

    bmaj = lambda a: jnp.swapaxes(a, 0, 1)
    wa, wb, wf = a_w_dw.shape[1], b_w_dw.shape[1], f_w_dw.shape[1]
    return (
        yp.reshape(bp, seq_len, d),
        bmaj(ys.reshape(n_steps, bs, d)),
        jnp.stack([z.reshape(bp, seq_len, d)[:, seq_len - (wa - 1):] for z in a_p]),
        jnp.stack([bmaj(s) for s in a_s]),
        jnp.stack([prompt_tail(t, wb) for t in b_p]),
        jnp.stack([bmaj(s) for s in b_s]),
        jnp.stack(c_p),
        jnp.stack([bmaj(v.reshape(n_steps, bs, d)) for v in c_s]),
        jnp.stack([prompt_tail(t, wf) for t in f_p]),
        jnp.stack([bmaj(s) for s in f_s]),
    )
```

```python
import functools

import jax
import jax.numpy as jnp
from jax import lax
from jax.experimental import pallas as pl
from jax.experimental.pallas import tpu as pltpu

EPS = 1e-6
F32 = jnp.float32
BF16 = jnp.bfloat16

ROW_TILE = 512
PROJ_ROW_TILE = 1024
COL_TILE = 512
CONV_COL_TILE = 256
CONV_ROW_CHUNK = 64
GATE_ROW_CHUNK = 64
SUBLANES = 8
VMEM_LIMIT = 56 * 1024 * 1024


def _params(*sem):
    return pltpu.CompilerParams(dimension_semantics=sem, vmem_limit_bytes=VMEM_LIMIT)


def _rmsnorm(x, g):
    return x * lax.rsqrt(jnp.mean(x * x, axis=-1, keepdims=True) + EPS) * g


def _layernorm(x, g, b):
    xc = x - jnp.mean(x, axis=-1, keepdims=True)
    return xc * lax.rsqrt(jnp.mean(xc * xc, axis=-1, keepdims=True) + EPS) * g + b


def _dot(a, b):
    return jnp.dot(a, b, preferred_element_type=F32)


def _round_up(n, m):
    return -(-n // m) * m


def _conv_slabs(slabs, w_ref, n_out):
    width = w_ref.shape[0]
    out = []
    for t in range(n_out):
        acc = None
        for k in range(width):
            term = w_ref[k:k + 1, :] * slabs[t + k]
            acc = term if acc is None else acc + term
        out.append(acc)
    return out


def _in_proj_kernel(h_ref, g_ref, w0_ref, w1_ref, b0_ref, b1_ref, *rest, kind):
    outs, xn_ref = rest[:-1], rest[-1]

    @pl.when(pl.program_id(1) == 0)
    def _():
        xn_ref[...] = _rmsnorm(h_ref[...], g_ref[...]).astype(BF16)

    xn = xn_ref[...]
    a = _dot(xn, w0_ref[...]) + b0_ref[...]
    b = _dot(xn, w1_ref[...]) + b1_ref[...]
    if kind == "glu":
        outs[0][...] = a * jax.nn.sigmoid(b)
    else:
        outs[0][...] = jax.nn.gelu(a)
        outs[1][...] = jax.nn.gelu(b)


def _in_proj(h, g, g_layer, w, b, layer, kind):
    m, d = h.shape
    tm, tn = min(PROJ_ROW_TILE, m), min(COL_TILE, d)
    nj = d // tn
    n_out = 1 if kind == "glu" else 2
    return pl.pallas_call(
        functools.partial(_in_proj_kernel, kind=kind),
        grid=(m // tm, nj),
        in_specs=[
            pl.BlockSpec((tm, d), lambda i, j: (i, 0)),
            pl.BlockSpec((None, 1, d), lambda i, j: (g_layer, 0, 0)),
            pl.BlockSpec((None, d, tn), lambda i, j: (layer, 0, j)),
            pl.BlockSpec((None, d, tn), lambda i, j: (layer, 0, j + nj)),
            pl.BlockSpec((None, 1, tn), lambda i, j: (layer, 0, j)),
            pl.BlockSpec((None, 1, tn), lambda i, j: (layer, 0, j + nj)),
        ],
        out_specs=[pl.BlockSpec((tm, tn), lambda i, j: (i, j))] * n_out,
        out_shape=[jax.ShapeDtypeStruct((m, d), F32)] * n_out,
        scratch_shapes=[pltpu.VMEM((tm, d), BF16)],
        compiler_params=_params("parallel", "arbitrary"),
        name=f"in_proj_{kind}",
    )(h, g, w, w, b, b)


def _gate(conv, bias, other, mode):
    if mode == "ffn":
        return jax.nn.silu(conv + bias) * other
    return other * conv


def _gated_prompt_kernel(*refs, n_in, mode, tiles_per_seq, row_chunk):
    h_ref, g_ref = refs[:2]
    w_in = refs[2:2 + n_in]
    wdw_ref, bdw_ref, w2_ref, ho_ref, tail_ref, xn_ref, ext_ref, oth_ref, hid_ref, carry_ref = refs[2 + n_in:]
    i, j = pl.program_id(0), pl.program_id(1)
    tm, tf = oth_ref.shape
    hist = SUBLANES
    width = wdw_ref.shape[0]
    base = hist - (width - 1)

    @pl.when(j == 0)
    def _():
        xn_ref[...] = _rmsnorm(h_ref[...], g_ref[...]).astype(BF16)
        ho_ref[...] = h_ref[...]

        @pl.when(i % tiles_per_seq == 0)
        def _():
            carry_ref[...] = jnp.zeros(carry_ref.shape, F32)

    xn = xn_ref[...]
    ext_ref[0:hist, :] = carry_ref[j]
    if mode == "ffn":
        ext_ref[hist:hist + tm, :] = _dot(xn, w_in[0][...])
        oth_ref[...] = _dot(xn, w_in[1][...])
    else:
        oth_ref[...] = _dot(xn, w_in[0][...])
        ext_ref[hist:hist + tm, :] = _dot(xn, w_in[1][...]) * _dot(xn, w_in[2][...])
    tail = ext_ref[tm:tm + hist, :]
    carry_ref[j] = tail
    tail_ref[...] = tail

    for r0 in range(0, tm, row_chunk):
        for c0 in range(0, tf, 128):
            cs = slice(c0, c0 + 128)
            conv = None
            for k in range(width):
                term = wdw_ref[k:k + 1, cs] * ext_ref[base + k + r0:base + k + r0 + row_chunk, cs]
                conv = term if conv is None else conv + term
            hid = _gate(conv, bdw_ref[:, cs], oth_ref[r0:r0 + row_chunk, cs], mode)
            hid_ref[r0:r0 + row_chunk, cs] = hid.astype(BF16)

    ho_ref[...] += _dot(hid_ref[...], w2_ref[...])


def _gated_sample_kernel(*refs, n_in, mode, n_steps):
    h_ref, g_ref = refs[:2]
    w_in = refs[2:2 + n_in]
    wdw_ref, bdw_ref, w2_ref, st_ref, ho_ref, ns_ref, xn_ref = refs[2 + n_in:]
    j = pl.program_id(0)
    n_hist = st_ref.shape[0]
    rb = h_ref.shape[0] // n_steps

    @pl.when(j == 0)
    def _():
        xn_ref[...] = _rmsnorm(h_ref[...], g_ref[...]).astype(BF16)
        ho_ref[...] = h_ref[...]

    xn = xn_ref[...]
    proj = [_dot(xn, w[...]) for w in w_in]
    v, other = (proj[0], proj[1]) if mode == "ffn" else (proj[1] * proj[2], proj[0])
    slabs = [st_ref[k] for k in range(n_hist)] + [v[t * rb:(t + 1) * rb, :] for t in range(n_steps)]
    for k in range(n_hist):
        ns_ref[k] = slabs[n_steps + k]
    conv = jnp.concatenate(_conv_slabs(slabs, wdw_ref, n_steps), axis=0)
    hid = _gate(conv, bdw_ref[...], other, mode)
    ho_ref[...] += _dot(hid.astype(BF16), w2_ref[...])


def _gated(h, g, g_layer, w_in_list, wdw, bdw, w2, layer, mode, *, seq_len=None, state=None, n_steps=None):
    m, d = h.shape
    hidden = w2.shape[1]
    tf = min(COL_TILE, hidden)
    nj = hidden // tf
    width = wdw.shape[1]
    n_in = len(w_in_list)
    sample = state is not None
    tm = m if sample else min(ROW_TILE, seq_len)
    if sample:
        row = lambda j: (0, 0)
        col = lambda off: (lambda j: (layer, 0, j + off))
        vec = lambda l: (lambda j: (l, 0, 0))
        hid_vec = lambda j: (layer, 0, j)
        w2_map = lambda j: (layer, j, 0)
        grid = (nj,)
    else:
        row = lambda i, j: (i, 0)
        col = lambda off: (lambda i, j: (layer, 0, j + off))
        vec = lambda l: (lambda i, j: (l, 0, 0))
        hid_vec = lambda i, j: (layer, 0, j)
        w2_map = lambda i, j: (layer, j, 0)
        grid = (m // tm, nj)
    in_specs = [pl.BlockSpec((tm, d), row), pl.BlockSpec((None, 1, d), vec(g_layer))]
    in_specs += [pl.BlockSpec((None, d, tf), col(off * nj)) for _, off in w_in_list]
    in_specs += [
        pl.BlockSpec((None, width, tf), hid_vec),
        pl.BlockSpec((None, 1, tf), hid_vec),
        pl.BlockSpec((None, tf, d), w2_map),
    ]
    args = [h, g] + [w for w, _ in w_in_list] + [wdw, bdw, w2]
    scratch = [pltpu.VMEM((tm, d), BF16)]
    if sample:
        batch = state.shape[2]
        in_specs.append(pl.BlockSpec((None, width - 1, batch, tf), lambda j: (layer, 0, 0, j)))
        args.append(state)
        out_specs = [pl.BlockSpec((tm, d), row), pl.BlockSpec((width - 1, batch, tf), lambda j: (0, 0, j))]
        out_shape = [jax.ShapeDtypeStruct((m, d), F32), jax.ShapeDtypeStruct((width - 1, batch, hidden), F32)]
        body = functools.partial(_gated_sample_kernel, n_in=n_in, mode=mode, n_steps=n_steps)
        sem = ("arbitrary",)
    else:
        assert seq_len % tm == 0 and width - 1 <= SUBLANES <= tm
        out_specs = [pl.BlockSpec((tm, d), row), pl.BlockSpec((None, SUBLANES, tf), lambda i, j: (i, 0, j))]
        out_shape = [jax.ShapeDtypeStruct((m, d), F32), jax.ShapeDtypeStruct((m // tm, SUBLANES, hidden), F32)]
        scratch += [pltpu.VMEM((SUBLANES + tm, tf), F32), pltpu.VMEM((tm, tf), F32), pltpu.VMEM((tm, tf), BF16),
                    pltpu.VMEM((nj, SUBLANES, tf), F32)]
        body = functools.partial(_gated_prompt_kernel, n_in=n_in, mode=mode, tiles_per_seq=seq_len // tm,
                                 row_chunk=min(GATE_ROW_CHUNK, tm))
        sem = ("arbitrary", "arbitrary")
    return pl.pallas_call(
        body, grid=grid, in_specs=in_specs, out_specs=out_specs, out_shape=out_shape,
        scratch_shapes=scratch, compiler_params=_params(*sem),
        name=f"gated_{mode}_{'sample' if sample else 'prompt'}",
    )(*args)


def _conformer_prompt_kernel(h_ref, hcol_ref, g_ref, wa_ref, wg_ref, ba_ref, bg_ref, wdw_ref, bdw_ref, lng_ref, lnb_ref,
                             wo_ref, bo_ref, ho_ref, ztail_ref, xn_ref, zext_ref, sh_ref, y_ref, s_ref,
                             *, tiles_per_seq, row_chunk):
    i, j = pl.program_id(0), pl.program_id(1)
    nb, tm, tn = y_ref.shape
    hist = zext_ref.shape[1] - tm
    width = wdw_ref.shape[0]
    base = hist - (width - 1)
    d = nb * tn

    def history_block(b):
        @pl.when(i % tiles_per_seq == 0)
        def _():
            zext_ref[b, 0:hist, :] = jnp.zeros((hist, tn), F32)

        @pl.when(i % tiles_per_seq != 0)
        def _():
            zext_ref[b, 0:hist, :] = zext_ref[b, tm:tm + hist, :]

    def in_proj_block(b):
        xn = xn_ref[...]
        a = _dot(xn, wa_ref[...]) + ba_ref[...]
        g = _dot(xn, wg_ref[...]) + bg_ref[...]
        zext_ref[b, hist:hist + tm, :] = a * jax.nn.sigmoid(g)

    def conv_block(b):
        n_shifted = hist + tm - SUBLANES
        for s in range(1, SUBLANES):
            sh_ref[s - 1, 0:n_shifted, :] = zext_ref[b, s:s + n_shifted, :]
        for r0 in range(0, tm, row_chunk):
            for c0 in range(0, tn, 128):
                cs = slice(c0, c0 + 128)
                acc = None
                for k in range(width):
                    q, s = divmod(base + k, SUBLANES)
                    rows = slice(r0 + q * SUBLANES, r0 + q * SUBLANES + row_chunk)
                    src = zext_ref[b, rows, cs] if s == 0 else sh_ref[s - 1, rows, cs]
                    term = wdw_ref[k:k + 1, cs] * src
                    acc = term if acc is None else acc + term
                y_ref[b, r0:r0 + row_chunk, cs] = acc

    @pl.when(j == 0)
    def _():
        xn_ref[...] = _rmsnorm(h_ref[...], g_ref[...]).astype(BF16)
        history_block(0)
        in_proj_block(0)

    for b in range(1, nb):
        @pl.when(j == b)
        def _(b=b):
            history_block(b)
            conv_block(b - 1)
            in_proj_block(b)

    @pl.when(j == nb)
    def _():
        conv_block(nb - 1)
        cols = [slice(b * tn, (b + 1) * tn) for b in range(nb)]
        yb = [y_ref[b] + bdw_ref[:, cols[b]] for b in range(nb)]
        mean = sum(jnp.sum(v, axis=-1, keepdims=True) for v in yb) / d
        yc = [v - mean for v in yb]
        var = sum(jnp.sum(v * v, axis=-1, keepdims=True) for v in yc) / d
        rstd = lax.rsqrt(var + EPS)
        for b in range(nb):
            s_ref[b] = jax.nn.silu(yc[b] * rstd * lng_ref[:, cols[b]] + lnb_ref[:, cols[b]]).astype(BF16)
            ztail_ref[b] = zext_ref[b, tm:tm + hist, :]

    @pl.when(j >= nb)
    def _():
        acc = hcol_ref[...] + bo_ref[...]
        for b in range(nb):
            acc = acc + _dot(s_ref[b], wo_ref[b * tn:(b + 1) * tn, :])
        ho_ref[...] = acc


def _conformer_prompt(h, wts, g_layer, layer, seq_len):
    m, d = h.shape
    width = wts["a_w_dw"].shape[1]
    tm, tn = min(ROW_TILE, seq_len), min(COL_TILE, d)
    nb = d // tn
    hist = _round_up(width - 1, SUBLANES)
    assert seq_len % tm == 0 and hist <= tm
    in_blk = lambda i, j: (layer, 0, jnp.minimum(j, nb - 1))
    in_blk_gate = lambda i, j: (layer, 0, jnp.minimum(j, nb - 1) + nb)
    out_blk = lambda i, j: (layer, 0, jnp.maximum(j - nb, 0))
    vec = lambda i, j: (layer, 0, 0)
    return pl.pallas_call(
        functools.partial(_conformer_prompt_kernel, tiles_per_seq=seq_len // tm, row_chunk=min(CONV_ROW_CHUNK, tm)),
        grid=(m // tm, 2 * nb),
        in_specs=[
            pl.BlockSpec((tm, d), lambda i, j: (i, 0)),
            pl.BlockSpec((tm, tn), lambda i, j: (i, jnp.maximum(j - nb, 0))),
            pl.BlockSpec((None, 1, d), lambda i, j: (g_layer, 0, 0)),
            pl.BlockSpec((None, d, tn), in_blk),
            pl.BlockSpec((None, d, tn), in_blk_gate),
            pl.BlockSpec((None, 1, tn), in_blk),
            pl.BlockSpec((None, 1, tn), in_blk_gate),
            pl.BlockSpec((None, width, tn), lambda i, j: (layer, 0, jnp.clip(j - 1, 0, nb - 1))),
            pl.BlockSpec((None, 1, d), vec),
            pl.BlockSpec((None, 1, d), vec),
            pl.BlockSpec((None, 1, d), vec),
            pl.BlockSpec((None, d, tn), out_blk),
            pl.BlockSpec((None, 1, tn), out_blk),
        ],
        out_specs=[
            pl.BlockSpec((tm, tn), lambda i, j: (i, jnp.maximum(j - nb, 0))),
            pl.BlockSpec((None, nb, hist, tn), lambda i, j: (i, 0, 0, 0)),
        ],
        out_shape=[jax.ShapeDtypeStruct((m, d), F32), jax.ShapeDtypeStruct((m // tm, nb, hist, tn), F32)],
        scratch_shapes=[
            pltpu.VMEM((tm, d), BF16),
            pltpu.VMEM((nb, hist + tm, tn), F32),
            pltpu.VMEM((SUBLANES - 1, hist + tm, tn), F32),
            pltpu.VMEM((nb, tm, tn), F32),
            pltpu.VMEM((nb, tm, tn), BF16),
        ],
        compiler_params=_params("arbitrary", "arbitrary"),
        name="conformer_prompt",
    )(h, h, wts["g_mix"], wts["a_w_in"], wts["a_w_in"], wts["a_b_in"], wts["a_b_in"], wts["a_w_dw"],
      wts["a_b_dw"], wts["a_ln_g"], wts["a_ln_b"], wts["a_w_out"], wts["a_b_out"])


def _dwconv_sample_kernel(z_ref, st_ref, w_ref, y_ref, ns_ref, *, n_steps):
    n_hist = st_ref.shape[0]
    rb = z_ref.shape[0] // n_steps
    slabs = [st_ref[k] for k in range(n_hist)] + [z_ref[t * rb:(t + 1) * rb, :] for t in range(n_steps)]
    for k in range(n_hist):
        ns_ref[k] = slabs[n_steps + k]
    y = _conv_slabs(slabs, w_ref, n_steps)
    for t in range(n_steps):
        y_ref[t * rb:(t + 1) * rb, :] = y[t]


def _dwconv_sample(z, state, w, layer, n_steps):
    m, d = z.shape
    width = w.shape[1]
    batch = state.shape[2]
    tc = min(CONV_COL_TILE, d)
    return pl.pallas_call(
        functools.partial(_dwconv_sample_kernel, n_steps=n_steps),
        grid=(d // tc,),
        in_specs=[
            pl.BlockSpec((m, tc), lambda c: (0, c)),
            pl.BlockSpec((None, width - 1, batch, tc), lambda c: (layer, 0, 0, c)),
            pl.BlockSpec((None, width, tc), lambda c: (layer, 0, c)),
        ],
        out_specs=[pl.BlockSpec((m, tc), lambda c: (0, c)), pl.BlockSpec((width - 1, batch, tc), lambda c: (0, 0, c))],
        out_shape=[jax.ShapeDtypeStruct((m, d), F32), jax.ShapeDtypeStruct((width - 1, batch, d), F32)],
        compiler_params=_params("parallel"),
        name="dwconv_sample",
    )(z, state, w)


def _out_proj_kernel(h_ref, w_ref, b_ref, *rest, kind, chunk, n_groups, n_steps):
    j = pl.program_id(1)
    if kind == "conf":
        y_ref, bdw_ref, lg_ref, lb_ref, o_ref, s_ref = rest
    elif kind == "gmlp_prompt":
        u_ref, v_ref, lg_ref, lb_ref, ws_ref, bs_ref, o_ref, vlast_ref, s_ref = rest
    else:
        u_ref, v_ref, lg_ref, lb_ref, coef_ref, bs_ref, o_ref, vln_ref, s_ref = rest

    @pl.when(j == 0)
    def _():
        if kind == "conf":
            s_ref[...] = jax.nn.silu(_layernorm(y_ref[...] + bdw_ref[...], lg_ref[...], lb_ref[...])).astype(BF16)
        elif kind == "gmlp_prompt":
            tm, d = v_ref.shape
            gd = d // n_groups
            vln = _layernorm(v_ref[...], lg_ref[...], lb_ref[...])
            vlast_ref[...] = vln[tm - chunk:, :]
            vb = vln.astype(BF16)
            rows = lax.broadcasted_iota(jnp.int32, (chunk, chunk), 0)
            cols = lax.broadcasted_iota(jnp.int32, (chunk, chunk), 1)
            tril = (rows >= cols).astype(F32)
            for g in range(n_groups):
                wg = (ws_ref[g] * tril).astype(BF16)
                for c in range(tm // chunk):
                    rs, cs = slice(c * chunk, (c + 1) * chunk), slice(g * gd, (g + 1) * gd)
                    z = _dot(wg, vb[rs, cs]) + bs_ref[:, cs]
                    s_ref[rs, cs] = (u_ref[rs, cs] * z).astype(BF16)
        else:
            rb = v_ref.shape[0] // n_steps
            vln = _layernorm(v_ref[...], lg_ref[...], lb_ref[...])
            vln_ref[...] = vln
            vb = vln.astype(BF16).astype(F32)
            for t in range(n_steps):
                z = bs_ref[t:t + 1, :]
                for s in range(t + 1):
                    coef = coef_ref[t * n_steps + s:t * n_steps + s + 1, :].astype(BF16).astype(F32)
                    z = z + coef * vb[s * rb:(s + 1) * rb, :]
                s_ref[t * rb:(t + 1) * rb, :] = (u_ref[t * rb:(t + 1) * rb, :] * z).astype(BF16)

    o_ref[...] = h_ref[...] + _dot(s_ref[...], w_ref[...]) + b_ref[...]


def _out_proj(h, w, b, layer, kind, tm, ins, in_specs_extra, extra_out=None, *, chunk=0, n_groups=0, n_steps=0):
    m, d = h.shape
    tn = min(COL_TILE, d)
    in_specs = [
        pl.BlockSpec((tm, tn), lambda i, j: (i, j)),
        pl.BlockSpec((None, d, tn), lambda i, j: (layer, 0, j)),
        pl.BlockSpec((None, 1, tn), lambda i, j: (layer, 0, j)),
    ] + in_specs_extra
    out_specs = [pl.BlockSpec((tm, tn), lambda i, j: (i, j))]
    out_shape = [jax.ShapeDtypeStruct((m, d), F32)]
    if extra_out is not None:
        out_specs.append(extra_out[0])
        out_shape.append(extra_out[1])
    return pl.pallas_call(
        functools.partial(_out_proj_kernel, kind=kind, chunk=chunk, n_groups=n_groups, n_steps=n_steps),
        grid=(m // tm, d // tn), in_specs=in_specs, out_specs=out_specs, out_shape=out_shape,
        scratch_shapes=[pltpu.VMEM((tm, d), BF16)],
        compiler_params=_params("arbitrary", "arbitrary"),
        name=f"out_proj_{kind}",
    )(h, w, b, *ins)


def _ple_kernel(h_ref, g_ref, wg_ref, p_ref, wp_ref, *rest, final, tn):
    if final:
        gf_ref, o_ref, y_ref = rest
    else:
        (o_ref,) = rest
    d = h_ref.shape[1]
    xn = _rmsnorm(h_ref[...], g_ref[...]).astype(BF16)
    pb = p_ref[...].astype(BF16)
    for c0 in range(0, d, tn):
        gate = jax.nn.sigmoid(_dot(xn, wg_ref[:, c0:c0 + tn]))
        o_ref[:, c0:c0 + tn] = h_ref[:, c0:c0 + tn] + gate * _dot(pb, wp_ref[:, c0:c0 + tn])
    if final:
        y_ref[...] = _rmsnorm(o_ref[...], gf_ref[...])


def _ple(h, g, wg, p, wp, layer, g_final=None):
    m, d = h.shape
    pdim = p.shape[2]
    tm = min(ROW_TILE, m)
    final = g_final is not None
    in_specs = [
        pl.BlockSpec((tm, d), lambda i: (i, 0)),
        pl.BlockSpec((None, 1, d), lambda i: (layer, 0, 0)),
        pl.BlockSpec((None, d, d), lambda i: (layer, 0, 0)),
        pl.BlockSpec((None, tm, pdim), lambda i: (layer, i, 0)),
        pl.BlockSpec((None, pdim, d), lambda i: (layer, 0, 0)),
    ]
    args = [h, g, wg, p, wp]
    out_specs = [pl.BlockSpec((tm, d), lambda i: (i, 0))]
    out_shape = [jax.ShapeDtypeStruct((m, d), F32)]
    if final:
        in_specs.append(pl.BlockSpec((1, d), lambda i: (0, 0)))
        args.append(g_final)
        out_specs.append(pl.BlockSpec((tm, d), lambda i: (i, 0)))
        out_shape.append(jax.ShapeDtypeStruct((m, d), F32))
    return pl.pallas_call(
        functools.partial(_ple_kernel, final=final, tn=min(COL_TILE, d)),
        grid=(m // tm,), in_specs=in_specs, out_specs=out_specs, out_shape=out_shape,
        compiler_params=_params("parallel"),
        name="ple_final" if final else "ple",
    )(*args)


def _trunk(h, p, wts, *, seq_len, n_steps, states):
    sample = states is not None
    depth = wts["g_mix"].shape[0]
    new_a, new_b, new_c, new_f = [], [], [], []
    y = None
    for i in range(depth):
        kind, l = i % 3, i // 3
        if kind == 0 and not sample:
            h, ns = _conformer_prompt(h, wts, i, l, seq_len)
            new_a.append(ns)
        elif kind == 0:
            (z,) = _in_proj(h, wts["g_mix"], i, wts["a_w_in"], wts["a_b_in"], l, "glu")
            conv, ns = _dwconv_sample(z, states[0], wts["a_w_dw"], l, n_steps)
            new_a.append(ns)
            m, d = h.shape
            vec = lambda i_, j_: (l, 0, 0)
            (h,) = _out_proj(
                h, wts["a_w_out"], wts["a_b_out"], l, "conf", m,
                [conv, wts["a_b_dw"], wts["a_ln_g"], wts["a_ln_b"]],
                [pl.BlockSpec((m, d), lambda i_, j_: (0, 0))] + [pl.BlockSpec((None, 1, d), vec)] * 3)
        elif kind == 1:
            h, ns = _gated(h, wts["g_mix"], i, [(wts["b_w_in"], 0), (wts["b_w_in"], 1), (wts["b_w_in"], 2)],
                           wts["b_w_dw"], wts["b_zero_bias"], wts["b_w_out"], l, "short",
                           seq_len=seq_len, state=states[1] if sample else None, n_steps=n_steps)
            new_b.append(ns)
        else:
            u, v = _in_proj(h, wts["g_mix"], i, wts["c_w_in"], wts["c_b_in"], l, "gelu2")
            m, d = h.shape
            chunk = wts["c_w_s"].shape[2]
            n_groups = wts["c_w_s"].shape[1]
            vec = lambda i_, j_: (l, 0, 0)
            if sample:
                full = pl.BlockSpec((m, d), lambda i_, j_: (0, 0))
                h, nv = _out_proj(
                    h, wts["c_w_out"], wts["c_b_out"], l, "gmlp_sample", m,
                    [u, v, wts["c_ln_g"], wts["c_ln_b"], wts["c_coef"], wts["c_bs_steps"]],
                    [full, full] + [pl.BlockSpec((None, 1, d), vec)] * 2
                    + [pl.BlockSpec((None, n_steps * n_steps, d), vec), pl.BlockSpec((None, n_steps, d), vec)],
                    (full, jax.ShapeDtypeStruct((m, d), F32)), n_steps=n_steps)
            else:
                tm = min(ROW_TILE, seq_len)
                assert seq_len % chunk == 0 and tm % chunk == 0
                tps = seq_len // tm
                rows = pl.BlockSpec((tm, d), lambda i_, j_: (i_, 0))
                h, nv = _out_proj(
                    h, wts["c_w_out"], wts["c_b_out"], l, "gmlp_prompt", tm,
                    [u, v, wts["c_ln_g"], wts["c_ln_b"], wts["c_w_s"], wts["c_bs_rows"]],
                    [rows, rows] + [pl.BlockSpec((None, 1, d), vec)] * 2
                    + [pl.BlockSpec((None, n_groups, chunk, chunk), lambda i_, j_: (l, 0, 0, 0)),
                       pl.BlockSpec((None, chunk, d), vec)],
                    (pl.BlockSpec((None, chunk, d), lambda i_, j_: (i_ // tps, 0, 0)),
                     jax.ShapeDtypeStruct((m // seq_len, chunk, d), F32)),
                    chunk=chunk, n_groups=n_groups)
            new_c.append(nv)
        h, nf = _gated(h, wts["g_ffn"], i, [(wts["f_w_gate"], 0), (wts["f_w_up"], 0)],
                       wts["f_w_dw"], wts["f_b_dw"], wts["f_w_down"], i, "ffn",
                       seq_len=seq_len, state=states[2] if sample else None, n_steps=n_steps)
        new_f.append(nf)
        if i == depth - 1:
            h, y = _ple(h, wts["g_ple"], wts["ple_w_gate"], p, wts["ple_w_proj"], i, wts["g_final"])
        else:
            (h,) = _ple(h, wts["g_ple"], wts["ple_w_gate"], p, wts["ple_w_proj"], i)
    return y, new_a, new_b, new_c, new_f


def kernel(x_prompt, x_sample, p_prompt, p_sample, state_conformer, state_shortconv, state_ffn, g_mix, g_ffn, g_ple, g_final, a_w_in, a_b_in, a_w_dw, a_b_dw, a_ln_g, a_ln_b, a_w_out, a_b_out, b_w_in, b_w_dw, b_w_out, c_w_in, c_b_in, c_ln_g, c_ln_b, c_w_s, c_b_s, c_w_out, c_b_out, f_w_gate, f_w_dw, f_b_dw, f_w_up, f_w_down, ple_w_proj, ple_w_gate):
    bp, seq_len, d = x_prompt.shape
    bs, n_steps, _ = x_sample.shape
    depth = g_mix.shape[0]
    n_groups, chunk = c_w_s.shape[1], c_w_s.shape[2]
    gd = d // n_groups
    assert n_steps <= chunk

    row3 = lambda a: a[:, None, :]
    wts = dict(
        g_mix=row3(g_mix), g_ffn=row3(g_ffn), g_ple=row3(g_ple), g_final=g_final[None, :],
        a_w_in=a_w_in.astype(BF16), a_b_in=row3(a_b_in), a_w_dw=a_w_dw, a_b_dw=row3(a_b_dw),
        a_ln_g=row3(a_ln_g), a_ln_b=row3(a_ln_b), a_w_out=a_w_out.astype(BF16), a_b_out=row3(a_b_out),
        b_w_in=b_w_in.astype(BF16), b_w_dw=b_w_dw, b_w_out=b_w_out.astype(BF16),
        b_zero_bias=jnp.zeros((b_w_dw.shape[0], 1, d), F32),
        c_w_in=c_w_in.astype(BF16), c_b_in=row3(c_b_in), c_ln_g=row3(c_ln_g), c_ln_b=row3(c_ln_b),
        c_w_s=c_w_s, c_w_out=c_w_out.astype(BF16), c_b_out=row3(c_b_out),
        c_bs_rows=jnp.repeat(jnp.swapaxes(c_b_s, 1, 2), gd, axis=2),
        c_bs_steps=jnp.repeat(jnp.swapaxes(c_b_s, 1, 2)[:, :n_steps], gd, axis=2),
        c_coef=jnp.repeat(jnp.transpose(c_w_s[:, :, :n_steps, :n_steps], (0, 2, 3, 1)).reshape(
            c_w_s.shape[0], n_steps * n_steps, n_groups), gd, axis=2),
        f_w_gate=f_w_gate.astype(BF16), f_w_dw=f_w_dw, f_b_dw=row3(f_b_dw),
        f_w_up=f_w_up.astype(BF16), f_w_down=f_w_down.astype(BF16),
        ple_w_proj=ple_w_proj.astype(BF16), ple_w_gate=ple_w_gate.astype(BF16),
    )

    hp = x_prompt.reshape(bp * seq_len, d)
    pp = p_prompt.reshape(depth, bp * seq_len, -1)
    yp, a_p, b_p, c_p, f_p = _trunk(hp, pp, wts, seq_len=seq_len, n_steps=None, states=None)

    tmaj = lambda a: jnp.swapaxes(a, 1, 2)
    hs = jnp.swapaxes(x_sample, 0, 1).reshape(n_steps * bs, d)
    ps = tmaj(p_sample).reshape(depth, n_steps * bs, -1)
    states = (tmaj(state_conformer), tmaj(state_shortconv), tmaj(state_ffn))
    ys, a_s, b_s, c_s, f_s = _trunk(hs, ps, wts, seq_len=None, n_steps=n_steps, states=states)

    tm = min(ROW_TILE, seq_len)
    tps = seq_len // tm

    def prompt_tail(tails, width):
        return tails[tps - 1::tps, SUBLANES - (width - 1):, :]

    def conformer_tail(tails):
        last = tails[tps - 1::tps]
        rows = jnp.swapaxes(last, 1, 2).reshape(bp, last.shape[2], d)
        return rows[:, rows.shape[1] - (wa - 1):]

    bmaj = lambda a: jnp.swapaxes(a, 0, 1)
    wa, wb, wf = a_w_dw.shape[1], b_w_dw.shape[1], f_w_dw.shape[1]
    return (
        yp.reshape(bp, seq_len, d),
        bmaj(ys.reshape(n_steps, bs, d)),
        jnp.stack([conformer_tail(t) for t in a_p]),
        jnp.stack([bmaj(s) for s in a_s]),
        jnp.stack([prompt_tail(t, wb) for t in b_p]),
        jnp.stack([bmaj(s) for s in b_s]),
        jnp.stack(c_p),
        jnp.stack([bmaj(v.reshape(n_steps, bs, d)) for v in c_s]),
        jnp.stack([prompt_tail(t, wf) for t in f_p]),
        jnp.stack([bmaj(s) for s in f_s]),
    )
```

```python
import functools

import jax
import jax.numpy as jnp
from jax import lax
from jax.experimental import pallas as pl
from jax.experimental.pallas import tpu as pltpu

EPS = 1e-6
F32 = jnp.float32
BF16 = jnp.bfloat16

ROW_TILE = 512
PROJ_ROW_TILE = 1024
COL_TILE = 512
CONV_COL_TILE = 256
SAMPLE_GATED_COL_TILE = 256
CONV_ROW_CHUNK = 32
GATE_ROW_CHUNK = 64
SUBLANES = 8
VMEM_LIMIT = 56 * 1024 * 1024


def _params(*sem):
    return pltpu.CompilerParams(dimension_semantics=sem, vmem_limit_bytes=VMEM_LIMIT)


def _rmsnorm(x, g):
    return x * lax.rsqrt(jnp.mean(x * x, axis=-1, keepdims=True) + EPS) * g


def _layernorm(x, g, b):
    xc = x - jnp.mean(x, axis=-1, keepdims=True)
    return xc * lax.rsqrt(jnp.mean(xc * xc, axis=-1, keepdims=True) + EPS) * g + b


def _dot(a, b):
    return jnp.dot(a, b, preferred_element_type=F32)


def _round_up(n, m):
    return -(-n // m) * m


def _weight_spec(operand, k, tn, col_of):
    arr, layer, part, n_parts = operand
    off = part * (arr.shape[2] // n_parts // tn)
    return pl.BlockSpec((None, k, tn), lambda *idx: (layer, 0, col_of(*idx) + off))


def _weight_row_spec(operand, tk, n):
    _, layer, _, n_parts = operand
    assert n_parts == 1
    return pl.BlockSpec((None, tk, n), lambda *idx: (layer, idx[-1], 0))


def _bf16_copy_spec(k, tn, col_of):
    return pl.BlockSpec((None, k, tn), lambda *idx: (0, 0, col_of(*idx)))


def _weight_tile(w_ref, copy_ref):
    if copy_ref is None:
        return w_ref[...]
    wb = w_ref[...].astype(BF16)
    copy_ref[...] = wb
    return wb


def _conv_slabs(slabs, w_ref, n_out):
    width = w_ref.shape[0]
    out = []
    for t in range(n_out):
        acc = None
        for k in range(width):
            term = w_ref[k:k + 1, :] * slabs[t + k]
            acc = term if acc is None else acc + term
        out.append(acc)
    return out


def _in_proj_kernel(h_ref, g_ref, w0_ref, w1_ref, b0_ref, b1_ref, *rest, kind, emit):
    outs, xn_ref = rest[:-1], rest[-1]

    @pl.when(pl.program_id(1) == 0)
    def _():
        xn_ref[...] = _rmsnorm(h_ref[...], g_ref[...]).astype(BF16)

    n_act = 1 if kind == "glu" else 2
    xn = xn_ref[...]
    w0 = _weight_tile(w0_ref, outs[n_act] if emit else None)
    w1 = _weight_tile(w1_ref, outs[n_act + 1] if emit else None)
    a = _dot(xn, w0) + b0_ref[...]
    b = _dot(xn, w1) + b1_ref[...]
    if kind == "glu":
        outs[0][...] = a * jax.nn.sigmoid(b)
    else:
        outs[0][...] = jax.nn.gelu(a)
        outs[1][...] = jax.nn.gelu(b)


def _in_proj(h, g, g_layer, w_parts, b, layer, kind, emit):
    m, d = h.shape
    tm, tn = min(PROJ_ROW_TILE, m), min(COL_TILE, d)
    nj = d // tn
    n_act = 1 if kind == "glu" else 2
    col = lambda i, j: j
    out_specs = [pl.BlockSpec((tm, tn), lambda i, j: (i, j))] * n_act
    out_shape = [jax.ShapeDtypeStruct((m, d), F32)] * n_act
    if emit:
        assert m == tm
        out_specs += [_bf16_copy_spec(d, tn, col)] * 2
        out_shape += [jax.ShapeDtypeStruct((1, d, d), BF16)] * 2
    return pl.pallas_call(
        functools.partial(_in_proj_kernel, kind=kind, emit=emit),
        grid=(m // tm, nj),
        in_specs=[
            pl.BlockSpec((tm, d), lambda i, j: (i, 0)),
            pl.BlockSpec((None, 1, d), lambda i, j: (g_layer, 0, 0)),
            _weight_spec(w_parts[0], d, tn, col),
            _weight_spec(w_parts[1], d, tn, col),
            pl.BlockSpec((None, 1, tn), lambda i, j: (layer, 0, j)),
            pl.BlockSpec((None, 1, tn), lambda i, j: (layer, 0, j + nj)),
        ],
        out_specs=out_specs, out_shape=out_shape,
        scratch_shapes=[pltpu.VMEM((tm, d), BF16)],
        compiler_params=_params("parallel", "arbitrary"),
        name=f"in_proj_{kind}",
    )(h, g, w_parts[0][0], w_parts[1][0], b, b)


def _gate(conv, bias, other, mode):
    if mode == "ffn":
        return jax.nn.silu(conv + bias) * other
    return other * conv


def _gated_prompt_kernel(*refs, n_in, mode, tiles_per_seq, row_chunk):
    h_ref, g_ref = refs[:2]
    w_in = refs[2:2 + n_in]
    wdw_ref, bdw_ref, w2_ref, ho_ref, tail_ref, xn_ref, ext_ref, oth_ref, hid_ref, carry_ref = refs[2 + n_in:]
    i, j = pl.program_id(0), pl.program_id(1)
    tm, tf = oth_ref.shape
    hist = SUBLANES
    width = wdw_ref.shape[0]
    base = hist - (width - 1)

    @pl.when(j == 0)
    def _():
        xn_ref[...] = _rmsnorm(h_ref[...], g_ref[...]).astype(BF16)
        ho_ref[...] = h_ref[...]

        @pl.when(i % tiles_per_seq == 0)
        def _():
            carry_ref[...] = jnp.zeros(carry_ref.shape, F32)

    xn = xn_ref[...]
    ext_ref[0:hist, :] = carry_ref[j]
    if mode == "ffn":
        ext_ref[hist:hist + tm, :] = _dot(xn, w_in[0][...])
        oth_ref[...] = _dot(xn, w_in[1][...])
    else:
        oth_ref[...] = _dot(xn, w_in[0][...])
        ext_ref[hist:hist + tm, :] = _dot(xn, w_in[1][...]) * _dot(xn, w_in[2][...])
    tail = ext_ref[tm:tm + hist, :]
    carry_ref[j] = tail
    tail_ref[...] = tail

    for r0 in range(0, tm, row_chunk):
        for c0 in range(0, tf, 128):
            cs = slice(c0, c0 + 128)
            conv = None
            for k in range(width):
                term = wdw_ref[k:k + 1, cs] * ext_ref[base + k + r0:base + k + r0 + row_chunk, cs]
                conv = term if conv is None else conv + term
            hid = _gate(conv, bdw_ref[:, cs], oth_ref[r0:r0 + row_chunk, cs], mode)
            hid_ref[r0:r0 + row_chunk, cs] = hid.astype(BF16)

    ho_ref[...] += _dot(hid_ref[...], w2_ref[...])


def _gated_sample_kernel(*refs, n_in, mode, n_steps):
    h_ref, g_ref = refs[:2]
    w_in = refs[2:2 + n_in]
    wdw_ref, bdw_ref, w2_ref, st_ref, ho_ref, ns_ref = refs[2 + n_in:8 + n_in]
    copies = refs[8 + n_in:9 + 2 * n_in]
    xn_ref = refs[-1]
    j = pl.program_id(0)
    n_hist = st_ref.shape[0]
    rb = h_ref.shape[0] // n_steps

    @pl.when(j == 0)
    def _():
        xn_ref[...] = _rmsnorm(h_ref[...], g_ref[...]).astype(BF16)
        ho_ref[...] = h_ref[...]

    xn = xn_ref[...]
    proj = [_dot(xn, _weight_tile(w, c)) for w, c in zip(w_in, copies[:n_in])]
    v, other = (proj[0], proj[1]) if mode == "ffn" else (proj[1] * proj[2], proj[0])
    slabs = [st_ref[k] for k in range(n_hist)] + [v[t * rb:(t + 1) * rb, :] for t in range(n_steps)]
    for k in range(n_hist):
        ns_ref[k] = slabs[n_steps + k]
    conv = jnp.concatenate(_conv_slabs(slabs, wdw_ref, n_steps), axis=0)
    hid = _gate(conv, bdw_ref[...], other, mode)
    ho_ref[...] += _dot(hid.astype(BF16), _weight_tile(w2_ref, copies[n_in]))


def _gated(h, g, g_layer, w_in_list, wdw, bdw, w2, layer, mode, *, seq_len=None, state=None, n_steps=None):
    m, d = h.shape
    hidden = wdw.shape[2]
    sample = state is not None
    tf = min(SAMPLE_GATED_COL_TILE if sample else COL_TILE, hidden)
    nj = hidden // tf
    width = wdw.shape[1]
    n_in = len(w_in_list)
    tm = m if sample else min(ROW_TILE, seq_len)
    if sample:
        row = lambda j: (0, 0)
        vec = lambda l: (lambda j: (l, 0, 0))
        hid_vec = lambda j: (layer, 0, j)
        grid = (nj,)
    else:
        row = lambda i, j: (i, 0)
        vec = lambda l: (lambda i, j: (l, 0, 0))
        hid_vec = lambda i, j: (layer, 0, j)
        grid = (m // tm, nj)
    col = lambda *idx: idx[-1]
    in_specs = [pl.BlockSpec((tm, d), row), pl.BlockSpec((None, 1, d), vec(g_layer))]
    in_specs += [_weight_spec(w, d, tf, col) for w in w_in_list]
    in_specs += [
        pl.BlockSpec((None, width, tf), hid_vec),
        pl.BlockSpec((None, 1, tf), hid_vec),
        _weight_row_spec(w2, tf, d),
    ]
    args = [h, g] + [w[0] for w in w_in_list] + [wdw, bdw, w2[0]]
    scratch = [pltpu.VMEM((tm, d), BF16)]
    if sample:
        batch = state.shape[2]
        in_specs.append(pl.BlockSpec((None, width - 1, batch, tf), lambda j: (layer, 0, 0, j)))
        args.append(state)
        out_specs = [pl.BlockSpec((tm, d), row), pl.BlockSpec((width - 1, batch, tf), lambda j: (0, 0, j))]
        out_shape = [jax.ShapeDtypeStruct((m, d), F32), jax.ShapeDtypeStruct((width - 1, batch, hidden), F32)]
        out_specs += [_bf16_copy_spec(d, tf, col)] * n_in + [pl.BlockSpec((None, tf, d), lambda j: (0, j, 0))]
        out_shape += [jax.ShapeDtypeStruct((1, d, hidden), BF16)] * n_in + [jax.ShapeDtypeStruct((1, hidden, d), BF16)]
        body = functools.partial(_gated_sample_kernel, n_in=n_in, mode=mode, n_steps=n_steps)
        sem = ("arbitrary",)
    else:
        assert seq_len % tm == 0 and width - 1 <= SUBLANES <= tm
        out_specs = [pl.BlockSpec((tm, d), row), pl.BlockSpec((None, SUBLANES, tf), lambda i, j: (i, 0, j))]
        out_shape = [jax.ShapeDtypeStruct((m, d), F32), jax.ShapeDtypeStruct((m // tm, SUBLANES, hidden), F32)]
        scratch += [pltpu.VMEM((SUBLANES + tm, tf), F32), pltpu.VMEM((tm, tf), F32), pltpu.VMEM((tm, tf), BF16),
                    pltpu.VMEM((nj, SUBLANES, tf), F32)]
        body = functools.partial(_gated_prompt_kernel, n_in=n_in, mode=mode, tiles_per_seq=seq_len // tm,
                                 row_chunk=min(GATE_ROW_CHUNK, tm))
        sem = ("arbitrary", "arbitrary")
    return pl.pallas_call(
        body, grid=grid, in_specs=in_specs, out_specs=out_specs, out_shape=out_shape,
        scratch_shapes=scratch, compiler_params=_params(*sem),
        name=f"gated_{mode}_{'sample' if sample else 'prompt'}",
    )(*args)


def _conformer_prompt_kernel(h_ref, hcol_ref, g_ref, wa_ref, wg_ref, ba_ref, bg_ref, wdw_ref, bdw_ref, lng_ref, lnb_ref,
                             wo_ref, bo_ref, ho_ref, ztail_ref, xn_ref, zext_ref, sh_ref, y_ref, s_ref,
                             *, tiles_per_seq, row_chunk):
    i, j = pl.program_id(0), pl.program_id(1)
    nb, tm, tn = y_ref.shape
    hist = zext_ref.shape[1] - tm
    width = wdw_ref.shape[0]
    base = hist - (width - 1)
    d = nb * tn

    def history_block(b):
        @pl.when(i % tiles_per_seq == 0)
        def _():
            zext_ref[b, 0:hist, :] = jnp.zeros((hist, tn), F32)

        @pl.when(i % tiles_per_seq != 0)
        def _():
            zext_ref[b, 0:hist, :] = zext_ref[b, tm:tm + hist, :]

    def in_proj_block(b):
        xn = xn_ref[...]
        a = _dot(xn, wa_ref[...]) + ba_ref[...]
        g = _dot(xn, wg_ref[...]) + bg_ref[...]
        zext_ref[b, hist:hist + tm, :] = a * jax.nn.sigmoid(g)

    def conv_block(b):
        n_shifted = hist + tm - SUBLANES
        for s in range(1, SUBLANES):
            sh_ref[s - 1, 0:n_shifted, :] = zext_ref[b, s:s + n_shifted, :]
        lane_cols = [slice(c0, c0 + 128) for c0 in range(0, tn, 128)]
        for r0 in range(0, tm, row_chunk):
            acc = [None] * len(lane_cols)
            for k in range(width):
                q, s = divmod(base + k, SUBLANES)
                rows = slice(r0 + q * SUBLANES, r0 + q * SUBLANES + row_chunk)
                for c, cs in enumerate(lane_cols):
                    src = zext_ref[b, rows, cs] if s == 0 else sh_ref[s - 1, rows, cs]
                    term = wdw_ref[k:k + 1, cs] * src
                    acc[c] = term if acc[c] is None else acc[c] + term
            for c, cs in enumerate(lane_cols):
                y_ref[b, r0:r0 + row_chunk, cs] = acc[c]

    @pl.when(j == 0)
    def _():
        xn_ref[...] = _rmsnorm(h_ref[...], g_ref[...]).astype(BF16)
        history_block(0)
        in_proj_block(0)

    for b in range(1, nb):
        @pl.when(j == b)
        def _(b=b):
            history_block(b)
            conv_block(b - 1)
            in_proj_block(b)

    @pl.when(j == nb)
    def _():
        conv_block(nb - 1)
        cols = [slice(b * tn, (b + 1) * tn) for b in range(nb)]
        yb = [y_ref[b] + bdw_ref[:, cols[b]] for b in range(nb)]
        mean = sum(jnp.sum(v, axis=-1, keepdims=True) for v in yb) / d
        yc = [v - mean for v in yb]
        var = sum(jnp.sum(v * v, axis=-1, keepdims=True) for v in yc) / d
        rstd = lax.rsqrt(var + EPS)
        for b in range(nb):
            s_ref[b] = jax.nn.silu(yc[b] * rstd * lng_ref[:, cols[b]] + lnb_ref[:, cols[b]]).astype(BF16)
            ztail_ref[b] = zext_ref[b, tm:tm + hist, :]

    @pl.when(j >= nb)
    def _():
        acc = hcol_ref[...] + bo_ref[...]
        for b in range(nb):
            acc = acc + _dot(s_ref[b], wo_ref[b * tn:(b + 1) * tn, :])
        ho_ref[...] = acc


def _conformer_prompt(h, wts, w_in_parts, w_out, g_layer, layer, seq_len):
    m, d = h.shape
    width = wts["a_w_dw"].shape[1]
    tm, tn = min(ROW_TILE, seq_len), min(COL_TILE, d)
    nb = d // tn
    n_tiles = m // tm
    hist = _round_up(width - 1, SUBLANES)
    assert seq_len % tm == 0 and hist <= tm
    in_col = lambda i, j: jnp.where(j < nb, j, 0)
    out_col = lambda i, j: jnp.maximum(j - nb, 0)
    in_blk = lambda i, j: (layer, 0, in_col(i, j))
    in_blk_gate = lambda i, j: (layer, 0, in_col(i, j) + nb)
    out_blk = lambda i, j: (layer, 0, out_col(i, j))
    vec = lambda i, j: (layer, 0, 0)
    return pl.pallas_call(
        functools.partial(_conformer_prompt_kernel, tiles_per_seq=seq_len // tm, row_chunk=min(CONV_ROW_CHUNK, tm)),
        grid=(n_tiles, 2 * nb),
        in_specs=[
            pl.BlockSpec((tm, d), lambda i, j: (jnp.minimum(jnp.where(j > 0, i + 1, i), n_tiles - 1), 0)),
            pl.BlockSpec((tm, tn), lambda i, j: (i, out_col(i, j))),
            pl.BlockSpec((None, 1, d), lambda i, j: (g_layer, 0, 0)),
            _weight_spec(w_in_parts[0], d, tn, in_col),
            _weight_spec(w_in_parts[1], d, tn, in_col),
            pl.BlockSpec((None, 1, tn), in_blk),
            pl.BlockSpec((None, 1, tn), in_blk_gate),
            pl.BlockSpec((None, width, tn), lambda i, j: (layer, 0, jnp.clip(j - 1, 0, nb - 1))),
            pl.BlockSpec((None, 1, d), vec),
            pl.BlockSpec((None, 1, d), vec),
            pl.BlockSpec((None, 1, d), vec),
            _weight_spec(w_out, d, tn, out_col),
            pl.BlockSpec((None, 1, tn), out_blk),
        ],
        out_specs=[
            pl.BlockSpec((tm, tn), lambda i, j: (i, jnp.maximum(j - nb, 0))),
            pl.BlockSpec((None, nb, hist, tn), lambda i, j: (i, 0, 0, 0)),
        ],
        out_shape=[jax.ShapeDtypeStruct((m, d), F32), jax.ShapeDtypeStruct((m // tm, nb, hist, tn), F32)],
        scratch_shapes=[
            pltpu.VMEM((tm, d), BF16),
            pltpu.VMEM((nb, hist + tm, tn), F32),
            pltpu.VMEM((SUBLANES - 1, hist + tm, tn), F32),
            pltpu.VMEM((nb, tm, tn), F32),
            pltpu.VMEM((nb, tm, tn), BF16),
        ],
        compiler_params=_params("arbitrary", "arbitrary"),
        name="conformer_prompt",
    )(h, h, wts["g_mix"], w_in_parts[0][0], w_in_parts[1][0], wts["a_b_in"], wts["a_b_in"], wts["a_w_dw"],
      wts["a_b_dw"], wts["a_ln_g"], wts["a_ln_b"], w_out[0], wts["a_b_out"])


def _dwconv_sample_kernel(z_ref, st_ref, w_ref, y_ref, ns_ref, *, n_steps):
    n_hist = st_ref.shape[0]
    rb = z_ref.shape[0] // n_steps
    slabs = [st_ref[k] for k in range(n_hist)] + [z_ref[t * rb:(t + 1) * rb, :] for t in range(n_steps)]
    for k in range(n_hist):
        ns_ref[k] = slabs[n_steps + k]
    y = _conv_slabs(slabs, w_ref, n_steps)
    for t in range(n_steps):
        y_ref[t * rb:(t + 1) * rb, :] = y[t]


def _dwconv_sample(z, state, w, layer, n_steps):
    m, d = z.shape
    width = w.shape[1]
    batch = state.shape[2]
    tc = min(CONV_COL_TILE, d)
    return pl.pallas_call(
        functools.partial(_dwconv_sample_kernel, n_steps=n_steps),
        grid=(d // tc,),
        in_specs=[
            pl.BlockSpec((m, tc), lambda c: (0, c)),
            pl.BlockSpec((None, width - 1, batch, tc), lambda c: (layer, 0, 0, c)),
            pl.BlockSpec((None, width, tc), lambda c: (layer, 0, c)),
        ],
        out_specs=[pl.BlockSpec((m, tc), lambda c: (0, c)), pl.BlockSpec((width - 1, batch, tc), lambda c: (0, 0, c))],
        out_shape=[jax.ShapeDtypeStruct((m, d), F32), jax.ShapeDtypeStruct((width - 1, batch, d), F32)],
        compiler_params=_params("parallel"),
        name="dwconv_sample",
    )(z, state, w)


def _out_proj_kernel(h_ref, w_ref, b_ref, *rest, kind, chunk, n_groups, n_steps, emit):
    j = pl.program_id(1)
    s_ref = rest[-1]
    copy_ref = rest[-2] if emit else None
    rest = rest[:-2] if emit else rest[:-1]
    if kind == "conf":
        y_ref, bdw_ref, lg_ref, lb_ref, o_ref = rest
    elif kind == "gmlp_prompt":
        u_ref, v_ref, lg_ref, lb_ref, ws_ref, bs_ref, o_ref, vlast_ref = rest
    else:
        u_ref, v_ref, lg_ref, lb_ref, coef_ref, bs_ref, o_ref, vln_ref = rest

    @pl.when(j == 0)
    def _():
        if kind == "conf":
            s_ref[...] = jax.nn.silu(_layernorm(y_ref[...] + bdw_ref[...], lg_ref[...], lb_ref[...])).astype(BF16)
        elif kind == "gmlp_prompt":
            tm, d = v_ref.shape
            gd = d // n_groups
            vln = _layernorm(v_ref[...], lg_ref[...], lb_ref[...])
            vlast_ref[...] = vln[tm - chunk:, :]
            vb = vln.astype(BF16)
            rows = lax.broadcasted_iota(jnp.int32, (chunk, chunk), 0)
            cols = lax.broadcasted_iota(jnp.int32, (chunk, chunk), 1)
            tril = (rows >= cols).astype(F32)
            for g in range(n_groups):
                wg = (ws_ref[g] * tril).astype(BF16)
                for c in range(tm // chunk):
                    rs, cs = slice(c * chunk, (c + 1) * chunk), slice(g * gd, (g + 1) * gd)
                    z = _dot(wg, vb[rs, cs]) + bs_ref[:, cs]
                    s_ref[rs, cs] = (u_ref[rs, cs] * z).astype(BF16)
        else:
            rb = v_ref.shape[0] // n_steps
            vln = _layernorm(v_ref[...], lg_ref[...], lb_ref[...])
            vln_ref[...] = vln
            vb = vln.astype(BF16).astype(F32)
            for t in range(n_steps):
                z = bs_ref[t:t + 1, :]
                for s in range(t + 1):
                    coef = coef_ref[t * n_steps + s:t * n_steps + s + 1, :].astype(BF16).astype(F32)
                    z = z + coef * vb[s * rb:(s + 1) * rb, :]
                s_ref[t * rb:(t + 1) * rb, :] = (u_ref[t * rb:(t + 1) * rb, :] * z).astype(BF16)

    o_ref[...] = h_ref[...] + _dot(s_ref[...], _weight_tile(w_ref, copy_ref)) + b_ref[...]


def _out_proj(h, w, b, layer, kind, tm, ins, in_specs_extra, extra_out=None, *, emit, chunk=0, n_groups=0, n_steps=0):
    m, d = h.shape
    tn = min(COL_TILE, d)
    col = lambda i, j: j
    in_specs = [
        pl.BlockSpec((tm, tn), lambda i, j: (i, j)),
        _weight_spec(w, d, tn, col),
        pl.BlockSpec((None, 1, tn), lambda i, j: (layer, 0, j)),
    ] + in_specs_extra
    out_specs = [pl.BlockSpec((tm, tn), lambda i, j: (i, j))]
    out_shape = [jax.ShapeDtypeStruct((m, d), F32)]
    if extra_out is not None:
        out_specs.append(extra_out[0])
        out_shape.append(extra_out[1])
    if emit:
        assert m == tm
        out_specs.append(_bf16_copy_spec(d, tn, col))
        out_shape.append(jax.ShapeDtypeStruct((1, d, d), BF16))
    return pl.pallas_call(
        functools.partial(_out_proj_kernel, kind=kind, chunk=chunk, n_groups=n_groups, n_steps=n_steps, emit=emit),
        grid=(m // tm, d // tn), in_specs=in_specs, out_specs=out_specs, out_shape=out_shape,
        scratch_shapes=[pltpu.VMEM((tm, d), BF16)],
        compiler_params=_params("arbitrary", "arbitrary"),
        name=f"out_proj_{kind}",
    )(h, w[0], b, *ins)


def _ple_kernel(h_ref, g_ref, wg_ref, p_ref, wp_ref, *rest, final, tn):
    if final:
        gf_ref, o_ref, y_ref = rest
    else:
        (o_ref,) = rest
    d = h_ref.shape[1]
    xn = _rmsnorm(h_ref[...], g_ref[...]).astype(BF16)
    pb = p_ref[...].astype(BF16)
    for c0 in range(0, d, tn):
        gate = jax.nn.sigmoid(_dot(xn, wg_ref[:, c0:c0 + tn]))
        o_ref[:, c0:c0 + tn] = h_ref[:, c0:c0 + tn] + gate * _dot(pb, wp_ref[:, c0:c0 + tn])
    if final:
        y_ref[...] = _rmsnorm(o_ref[...], gf_ref[...])


def _ple(h, g, wg, p, wp, layer, g_final=None):
    m, d = h.shape
    pdim = p.shape[2]
    tm = min(ROW_TILE, m)
    final = g_final is not None
    whole = lambda i: 0
    in_specs = [
        pl.BlockSpec((tm, d), lambda i: (i, 0)),
        pl.BlockSpec((None, 1, d), lambda i: (layer, 0, 0)),
        _weight_spec(wg, d, d, whole),
        pl.BlockSpec((None, tm, pdim), lambda i: (layer, i, 0)),
        _weight_spec(wp, pdim, d, whole),
    ]
    args = [h, g, wg[0], p, wp[0]]
    out_specs = [pl.BlockSpec((tm, d), lambda i: (i, 0))]
    out_shape = [jax.ShapeDtypeStruct((m, d), F32)]
    if final:
        in_specs.append(pl.BlockSpec((1, d), lambda i: (0, 0)))
        args.append(g_final)
        out_specs.append(pl.BlockSpec((tm, d), lambda i: (i, 0)))
        out_shape.append(jax.ShapeDtypeStruct((m, d), F32))
    return pl.pallas_call(
        functools.partial(_ple_kernel, final=final, tn=min(COL_TILE, d)),
        grid=(m // tm,), in_specs=in_specs, out_specs=out_specs, out_shape=out_shape,
        compiler_params=_params("parallel"),
        name="ple_final" if final else "ple",
    )(*args)


def _ple_sample_kernel(h_ref, hcol_ref, g_ref, wg_ref, p_ref, wp_ref, o_ref, wg_copy_ref, wp_copy_ref, xn_ref):
    @pl.when(pl.program_id(0) == 0)
    def _():
        xn_ref[...] = _rmsnorm(h_ref[...], g_ref[...]).astype(BF16)

    gate = jax.nn.sigmoid(_dot(xn_ref[...], _weight_tile(wg_ref, wg_copy_ref)))
    o_ref[...] = hcol_ref[...] + gate * _dot(p_ref[...].astype(BF16), _weight_tile(wp_ref, wp_copy_ref))


def _ple_sample(h, g, wg, p, wp, layer):
    m, d = h.shape
    pdim = p.shape[2]
    tn = min(COL_TILE, d)
    col = lambda j: j
    return pl.pallas_call(
        _ple_sample_kernel,
        grid=(d // tn,),
        in_specs=[
            pl.BlockSpec((m, d), lambda j: (0, 0)),
            pl.BlockSpec((m, tn), lambda j: (0, j)),
            pl.BlockSpec((None, 1, d), lambda j: (layer, 0, 0)),
            _weight_spec(wg, d, tn, col),
            pl.BlockSpec((None, m, pdim), lambda j: (layer, 0, 0)),
            _weight_spec(wp, pdim, tn, col),
        ],
        out_specs=[pl.BlockSpec((m, tn), lambda j: (0, j)), _bf16_copy_spec(d, tn, col), _bf16_copy_spec(pdim, tn, col)],
        out_shape=[jax.ShapeDtypeStruct((m, d), F32), jax.ShapeDtypeStruct((1, d, d), BF16),
                   jax.ShapeDtypeStruct((1, pdim, d), BF16)],
        scratch_shapes=[pltpu.VMEM((m, d), BF16)],
        compiler_params=_params("arbitrary"),
        name="ple_sample",
    )(h, h, g, wg[0], p, wp[0])


def _final_norm_kernel(h_ref, g_ref, y_ref):
    y_ref[...] = _rmsnorm(h_ref[...], g_ref[...])


def _final_norm(h, g):
    m, d = h.shape
    tm = min(ROW_TILE, m)
    return pl.pallas_call(
        _final_norm_kernel,
        grid=(m // tm,),
        in_specs=[pl.BlockSpec((tm, d), lambda i: (i, 0)), pl.BlockSpec((1, d), lambda i: (0, 0))],
        out_specs=pl.BlockSpec((tm, d), lambda i: (i, 0)),
        out_shape=jax.ShapeDtypeStruct((m, d), F32),
        compiler_params=_params("parallel"),
        name="final_norm",
    )(h, g)


def _trunk(h, p, wts, weight, *, seq_len, n_steps, states):
    sample = states is not None
    depth = wts["g_mix"].shape[0]
    new_a, new_b, new_c, new_f, copies = [], [], [], [], {}
    for i in range(depth):
        kind, l = i % 3, i // 3
        m, d = h.shape
        vec = lambda i_, j_, l=l: (l, 0, 0)
        if kind == 0 and not sample:
            h, ns = _conformer_prompt(h, wts, [weight("a_w_in", l, 0, 2), weight("a_w_in", l, 1, 2)],
                                      weight("a_w_out", l), i, l, seq_len)
            new_a.append(ns)
        elif kind == 0:
            z, wa_copy, wg_copy = _in_proj(h, wts["g_mix"], i, [weight("a_w_in", l, 0, 2), weight("a_w_in", l, 1, 2)],
                                           wts["a_b_in"], l, "glu", True)
            conv, ns = _dwconv_sample(z, states[0], wts["a_w_dw"], l, n_steps)
            new_a.append(ns)
            h, wo_copy = _out_proj(
                h, weight("a_w_out", l), wts["a_b_out"], l, "conf", m,
                [conv, wts["a_b_dw"], wts["a_ln_g"], wts["a_ln_b"]],
                [pl.BlockSpec((m, d), lambda i_, j_: (0, 0))] + [pl.BlockSpec((None, 1, d), vec)] * 3, emit=True)
            copies["a_w_in", l], copies["a_w_out", l] = (wa_copy, wg_copy), (wo_copy,)
        elif kind == 1:
            res = _gated(h, wts["g_mix"], i, [weight("b_w_in", l, part, 3) for part in range(3)],
                         wts["b_w_dw"], wts["b_zero_bias"], weight("b_w_out", l), l, "short",
                         seq_len=seq_len, state=states[1] if sample else None, n_steps=n_steps)
            h, ns = res[:2]
            new_b.append(ns)
            if sample:
                copies["b_w_in", l], copies["b_w_out", l] = tuple(res[2:5]), (res[5],)
        else:
            res = _in_proj(h, wts["g_mix"], i, [weight("c_w_in", l, 0, 2), weight("c_w_in", l, 1, 2)],
                           wts["c_b_in"], l, "gelu2", sample)
            u, v = res[:2]
            chunk = wts["c_w_s"].shape[2]
            n_groups = wts["c_w_s"].shape[1]
            if sample:
                full = pl.BlockSpec((m, d), lambda i_, j_: (0, 0))
                h, nv, wo_copy = _out_proj(
                    h, weight("c_w_out", l), wts["c_b_out"], l, "gmlp_sample", m,
                    [u, v, wts["c_ln_g"], wts["c_ln_b"], wts["c_coef"], wts["c_bs_steps"]],
                    [full, full] + [pl.BlockSpec((None, 1, d), vec)] * 2
                    + [pl.BlockSpec((None, n_steps * n_steps, d), vec), pl.BlockSpec((None, n_steps, d), vec)],
                    (full, jax.ShapeDtypeStruct((m, d), F32)), emit=True, n_steps=n_steps)
                copies["c_w_in", l], copies["c_w_out", l] = tuple(res[2:4]), (wo_copy,)
            else:
                tm = min(ROW_TILE, seq_len)
                assert seq_len % chunk == 0 and tm % chunk == 0
                tps = seq_len // tm
                rows = pl.BlockSpec((tm, d), lambda i_, j_: (i_, 0))
                h, nv = _out_proj(
                    h, weight("c_w_out", l), wts["c_b_out"], l, "gmlp_prompt", tm,
                    [u, v, wts["c_ln_g"], wts["c_ln_b"], wts["c_w_s"], wts["c_bs_rows"]],
                    [rows, rows] + [pl.BlockSpec((None, 1, d), vec)] * 2
                    + [pl.BlockSpec((None, n_groups, chunk, chunk), lambda i_, j_, l=l: (l, 0, 0, 0)),
                       pl.BlockSpec((None, chunk, d), vec)],
                    (pl.BlockSpec((None, chunk, d), lambda i_, j_: (i_ // tps, 0, 0)),
                     jax.ShapeDtypeStruct((m // seq_len, chunk, d), F32)),
                    emit=False, chunk=chunk, n_groups=n_groups)
            new_c.append(nv)
        res = _gated(h, wts["g_ffn"], i, [weight("f_w_gate", i), weight("f_w_up", i)],
                     wts["f_w_dw"], wts["f_b_dw"], weight("f_w_down", i), i, "ffn",
                     seq_len=seq_len, state=states[2] if sample else None, n_steps=n_steps)
        h, nf = res[:2]
        new_f.append(nf)
        if sample:
            copies["f_w_gate", i], copies["f_w_up", i], copies["f_w_down", i] = (res[2],), (res[3],), (res[4],)
            h, wg_copy, wp_copy = _ple_sample(h, wts["g_ple"], weight("ple_w_gate", i), p, weight("ple_w_proj", i), i)
            copies["ple_w_gate", i], copies["ple_w_proj", i] = (wg_copy,), (wp_copy,)
            if i == depth - 1:
                y = _final_norm(h, wts["g_final"])
        elif i == depth - 1:
            h, y = _ple(h, wts["g_ple"], weight("ple_w_gate", i), p, weight("ple_w_proj", i), i, wts["g_final"])
        else:
            (h,) = _ple(h, wts["g_ple"], weight("ple_w_gate", i), p, weight("ple_w_proj", i), i)
    return y, new_a, new_b, new_c, new_f, copies


def kernel(x_prompt, x_sample, p_prompt, p_sample, state_conformer, state_shortconv, state_ffn, g_mix, g_ffn, g_ple, g_final, a_w_in, a_b_in, a_w_dw, a_b_dw, a_ln_g, a_ln_b, a_w_out, a_b_out, b_w_in, b_w_dw, b_w_out, c_w_in, c_b_in, c_ln_g, c_ln_b, c_w_s, c_b_s, c_w_out, c_b_out, f_w_gate, f_w_dw, f_b_dw, f_w_up, f_w_down, ple_w_proj, ple_w_gate):
    bp, seq_len, d = x_prompt.shape
    bs, n_steps, _ = x_sample.shape
    depth = g_mix.shape[0]
    n_groups, chunk = c_w_s.shape[1], c_w_s.shape[2]
    gd = d // n_groups
    assert n_steps <= chunk

    row3 = lambda a: a[:, None, :]
    wts = dict(
        g_mix=row3(g_mix), g_ffn=row3(g_ffn), g_ple=row3(g_ple), g_final=g_final[None, :],
        a_b_in=row3(a_b_in), a_w_dw=a_w_dw, a_b_dw=row3(a_b_dw),
        a_ln_g=row3(a_ln_g), a_ln_b=row3(a_ln_b), a_b_out=row3(a_b_out),
        b_w_dw=b_w_dw, b_zero_bias=jnp.zeros((b_w_dw.shape[0], 1, d), F32),
        c_b_in=row3(c_b_in), c_ln_g=row3(c_ln_g), c_ln_b=row3(c_ln_b), c_w_s=c_w_s, c_b_out=row3(c_b_out),
        c_bs_rows=jnp.repeat(jnp.swapaxes(c_b_s, 1, 2), gd, axis=2),
        c_bs_steps=jnp.repeat(jnp.swapaxes(c_b_s, 1, 2)[:, :n_steps], gd, axis=2),
        c_coef=jnp.repeat(jnp.transpose(c_w_s[:, :, :n_steps, :n_steps], (0, 2, 3, 1)).reshape(
            c_w_s.shape[0], n_steps * n_steps, n_groups), gd, axis=2),
        f_w_dw=f_w_dw, f_b_dw=row3(f_b_dw),
    )
    big = dict(a_w_in=a_w_in, a_w_out=a_w_out, b_w_in=b_w_in, b_w_out=b_w_out, c_w_in=c_w_in, c_w_out=c_w_out,
               f_w_gate=f_w_gate, f_w_up=f_w_up, f_w_down=f_w_down, ple_w_gate=ple_w_gate, ple_w_proj=ple_w_proj)

    tmaj = lambda a: jnp.swapaxes(a, 1, 2)
    hs = jnp.swapaxes(x_sample, 0, 1).reshape(n_steps * bs, d)
    ps = tmaj(p_sample).reshape(depth, n_steps * bs, -1)
    states = (tmaj(state_conformer), tmaj(state_shortconv), tmaj(state_ffn))
    ys, a_s, b_s, c_s, f_s, copies = _trunk(
        hs, ps, wts, lambda name, layer, part=0, n_parts=1: (big[name], layer, part, n_parts),
        seq_len=None, n_steps=n_steps, states=states)

    hp = x_prompt.reshape(bp * seq_len, d)
    pp = p_prompt.reshape(depth, bp * seq_len, -1)
    yp, a_p, b_p, c_p, f_p, _ = _trunk(
        hp, pp, wts, lambda name, layer, part=0, n_parts=1: (copies[name, layer][part], 0, 0, 1),
        seq_len=seq_len, n_steps=None, states=None)

    tm = min(ROW_TILE, seq_len)
    tps = seq_len // tm

    def prompt_tail(tails, width):
        return tails[tps - 1::tps, SUBLANES - (width - 1):, :]

    def conformer_tail(tails):
        last = tails[tps - 1::tps]
        rows = jnp.swapaxes(last, 1, 2).reshape(bp, last.shape[2], d)
        return rows[:, rows.shape[1] - (wa - 1):]

    bmaj = lambda a: jnp.swapaxes(a, 0, 1)
    wa, wb, wf = a_w_dw.shape[1], b_w_dw.shape[1], f_w_dw.shape[1]
    return (
        yp.reshape(bp, seq_len, d),
        bmaj(ys.reshape(n_steps, bs, d)),
        jnp.stack([conformer_tail(t) for t in a_p]),
        jnp.stack([bmaj(s) for s in a_s]),
        jnp.stack([prompt_tail(t, wb) for t in b_p]),
        jnp.stack([bmaj(s) for s in b_s]),
        jnp.stack(c_p),
        jnp.stack([bmaj(v.reshape(n_steps, bs, d)) for v in c_s]),
        jnp.stack([prompt_tail(t, wf) for t in f_p]),
        jnp.stack([bmaj(s) for s in f_s]),
    )
```

```python
import functools

import jax
import jax.numpy as jnp
from jax import lax
from jax.experimental import pallas as pl
from jax.experimental.pallas import tpu as pltpu

EPS = 1e-6
F32 = jnp.float32
BF16 = jnp.bfloat16

ROW_TILE = 512
PROJ_ROW_TILE = 1024
COL_TILE = 512
CONFORMER_ROW_TILE = 256
CONV_COL_TILE = 256
SAMPLE_GATED_COL_TILE = 256
CONV_ROW_CHUNK = 32
GATE_ROW_CHUNK = 64
SUBLANES = 8
VMEM_LIMIT = 56 * 1024 * 1024


def _params(*sem):
    return pltpu.CompilerParams(dimension_semantics=sem, vmem_limit_bytes=VMEM_LIMIT)


def _rmsnorm(x, g):
    return x * lax.rsqrt(jnp.mean(x * x, axis=-1, keepdims=True) + EPS) * g


def _layernorm(x, g, b):
    xc = x - jnp.mean(x, axis=-1, keepdims=True)
    return xc * lax.rsqrt(jnp.mean(xc * xc, axis=-1, keepdims=True) + EPS) * g + b


def _dot(a, b):
    return jnp.dot(a, b, preferred_element_type=F32)


def _round_up(n, m):
    return -(-n // m) * m


def _weight_spec(operand, k, tn, col_of):
    arr, layer, part, n_parts = operand
    off = part * (arr.shape[2] // n_parts // tn)
    return pl.BlockSpec((None, k, tn), lambda *idx: (layer, 0, col_of(*idx) + off))


def _weight_row_spec(operand, tk, n):
    _, layer, _, n_parts = operand
    assert n_parts == 1
    return pl.BlockSpec((None, tk, n), lambda *idx: (layer, idx[-1], 0))


def _bf16_copy_spec(k, tn, col_of):
    return pl.BlockSpec((None, k, tn), lambda *idx: (0, 0, col_of(*idx)))


def _weight_tile(w_ref, copy_ref):
    if copy_ref is None:
        return w_ref[...]
    wb = w_ref[...].astype(BF16)
    copy_ref[...] = wb
    return wb


def _conv_slabs(slabs, w_ref, n_out):
    width = w_ref.shape[0]
    out = []
    for t in range(n_out):
        acc = None
        for k in range(width):
            term = w_ref[k:k + 1, :] * slabs[t + k]
            acc = term if acc is None else acc + term
        out.append(acc)
    return out


def _in_proj_kernel(h_ref, g_ref, w0_ref, w1_ref, b0_ref, b1_ref, *rest, kind, emit):
    outs, xn_ref = rest[:-1], rest[-1]

    @pl.when(pl.program_id(1) == 0)
    def _():
        xn_ref[...] = _rmsnorm(h_ref[...], g_ref[...]).astype(BF16)

    n_act = 1 if kind == "glu" else 2
    xn = xn_ref[...]
    w0 = _weight_tile(w0_ref, outs[n_act] if emit else None)
    w1 = _weight_tile(w1_ref, outs[n_act + 1] if emit else None)
    a = _dot(xn, w0) + b0_ref[...]
    b = _dot(xn, w1) + b1_ref[...]
    if kind == "glu":
        outs[0][...] = a * jax.nn.sigmoid(b)
    else:
        outs[0][...] = jax.nn.gelu(a)
        outs[1][...] = jax.nn.gelu(b)


def _in_proj(h, g, g_layer, w_parts, b, layer, kind, emit):
    m, d = h.shape
    tm, tn = min(PROJ_ROW_TILE, m), min(COL_TILE, d)
    nj = d // tn
    n_act = 1 if kind == "glu" else 2
    col = lambda i, j: j
    out_specs = [pl.BlockSpec((tm, tn), lambda i, j: (i, j))] * n_act
    out_shape = [jax.ShapeDtypeStruct((m, d), F32)] * n_act
    if emit:
        assert m == tm
        out_specs += [_bf16_copy_spec(d, tn, col)] * 2
        out_shape += [jax.ShapeDtypeStruct((1, d, d), BF16)] * 2
    return pl.pallas_call(
        functools.partial(_in_proj_kernel, kind=kind, emit=emit),
        grid=(m // tm, nj),
        in_specs=[
            pl.BlockSpec((tm, d), lambda i, j: (i, 0)),
            pl.BlockSpec((None, 1, d), lambda i, j: (g_layer, 0, 0)),
            _weight_spec(w_parts[0], d, tn, col),
            _weight_spec(w_parts[1], d, tn, col),
            pl.BlockSpec((None, 1, tn), lambda i, j: (layer, 0, j)),
            pl.BlockSpec((None, 1, tn), lambda i, j: (layer, 0, j + nj)),
        ],
        out_specs=out_specs, out_shape=out_shape,
        scratch_shapes=[pltpu.VMEM((tm, d), BF16)],
        compiler_params=_params("parallel", "arbitrary"),
        name=f"in_proj_{kind}",
    )(h, g, w_parts[0][0], w_parts[1][0], b, b)


def _gate(conv, bias, other, mode):
    if mode == "ffn":
        return jax.nn.silu(conv + bias) * other
    return other * conv


def _gated_prompt_kernel(*refs, n_in, mode, tiles_per_seq, row_chunk):
    h_ref, g_ref = refs[:2]
    w_in = refs[2:2 + n_in]
    wdw_ref, bdw_ref, w2_ref, ho_ref, tail_ref, xn_ref, ext_ref, oth_ref, hid_ref, carry_ref = refs[2 + n_in:]
    i, j = pl.program_id(0), pl.program_id(1)
    tm, tf = oth_ref.shape
    hist = SUBLANES
    width = wdw_ref.shape[0]
    base = hist - (width - 1)

    @pl.when(j == 0)
    def _():
        xn_ref[...] = _rmsnorm(h_ref[...], g_ref[...]).astype(BF16)
        ho_ref[...] = h_ref[...]

        @pl.when(i % tiles_per_seq == 0)
        def _():
            carry_ref[...] = jnp.zeros(carry_ref.shape, F32)

    xn = xn_ref[...]
    ext_ref[0:hist, :] = carry_ref[j]
    if mode == "ffn":
        ext_ref[hist:hist + tm, :] = _dot(xn, w_in[0][...])
        oth_ref[...] = _dot(xn, w_in[1][...])
    else:
        oth_ref[...] = _dot(xn, w_in[0][...])
        ext_ref[hist:hist + tm, :] = _dot(xn, w_in[1][...]) * _dot(xn, w_in[2][...])
    tail = ext_ref[tm:tm + hist, :]
    carry_ref[j] = tail
    tail_ref[...] = tail

    for r0 in range(0, tm, row_chunk):
        for c0 in range(0, tf, 128):
            cs = slice(c0, c0 + 128)
            conv = None
            for k in range(width):
                term = wdw_ref[k:k + 1, cs] * ext_ref[base + k + r0:base + k + r0 + row_chunk, cs]
                conv = term if conv is None else conv + term
            hid = _gate(conv, bdw_ref[:, cs], oth_ref[r0:r0 + row_chunk, cs], mode)
            hid_ref[r0:r0 + row_chunk, cs] = hid.astype(BF16)

    ho_ref[...] += _dot(hid_ref[...], w2_ref[...])


def _gated_sample_kernel(*refs, n_in, mode, n_steps):
    h_ref, g_ref = refs[:2]
    w_in = refs[2:2 + n_in]
    wdw_ref, bdw_ref, w2_ref, st_ref, ho_ref, ns_ref = refs[2 + n_in:8 + n_in]
    copies = refs[8 + n_in:9 + 2 * n_in]
    xn_ref = refs[-1]
    j = pl.program_id(0)
    n_hist = st_ref.shape[0]
    rb = h_ref.shape[0] // n_steps

    @pl.when(j == 0)
    def _():
        xn_ref[...] = _rmsnorm(h_ref[...], g_ref[...]).astype(BF16)
        ho_ref[...] = h_ref[...]

    xn = xn_ref[...]
    proj = [_dot(xn, _weight_tile(w, c)) for w, c in zip(w_in, copies[:n_in])]
    v, other = (proj[0], proj[1]) if mode == "ffn" else (proj[1] * proj[2], proj[0])
    slabs = [st_ref[k] for k in range(n_hist)] + [v[t * rb:(t + 1) * rb, :] for t in range(n_steps)]
    for k in range(n_hist):
        ns_ref[k] = slabs[n_steps + k]
    conv = jnp.concatenate(_conv_slabs(slabs, wdw_ref, n_steps), axis=0)
    hid = _gate(conv, bdw_ref[...], other, mode)
    ho_ref[...] += _dot(hid.astype(BF16), _weight_tile(w2_ref, copies[n_in]))


def _gated(h, g, g_layer, w_in_list, wdw, bdw, w2, layer, mode, *, seq_len=None, state=None, n_steps=None):
    m, d = h.shape
    hidden = wdw.shape[2]
    sample = state is not None
    tf = min(SAMPLE_GATED_COL_TILE if sample else COL_TILE, hidden)
    nj = hidden // tf
    width = wdw.shape[1]
    n_in = len(w_in_list)
    tm = m if sample else min(ROW_TILE, seq_len)
    if sample:
        row = lambda j: (0, 0)
        vec = lambda l: (lambda j: (l, 0, 0))
        hid_vec = lambda j: (layer, 0, j)
        grid = (nj,)
    else:
        row = lambda i, j: (i, 0)
        vec = lambda l: (lambda i, j: (l, 0, 0))
        hid_vec = lambda i, j: (layer, 0, j)
        grid = (m // tm, nj)
    col = lambda *idx: idx[-1]
    in_specs = [pl.BlockSpec((tm, d), row), pl.BlockSpec((None, 1, d), vec(g_layer))]
    in_specs += [_weight_spec(w, d, tf, col) for w in w_in_list]
    in_specs += [
        pl.BlockSpec((None, width, tf), hid_vec),
        pl.BlockSpec((None, 1, tf), hid_vec),
        _weight_row_spec(w2, tf, d),
    ]
    args = [h, g] + [w[0] for w in w_in_list] + [wdw, bdw, w2[0]]
    scratch = [pltpu.VMEM((tm, d), BF16)]
    if sample:
        batch = state.shape[2]
        in_specs.append(pl.BlockSpec((None, width - 1, batch, tf), lambda j: (layer, 0, 0, j)))
        args.append(state)
        out_specs = [pl.BlockSpec((tm, d), row), pl.BlockSpec((width - 1, batch, tf), lambda j: (0, 0, j))]
        out_shape = [jax.ShapeDtypeStruct((m, d), F32), jax.ShapeDtypeStruct((width - 1, batch, hidden), F32)]
        out_specs += [_bf16_copy_spec(d, tf, col)] * n_in + [pl.BlockSpec((None, tf, d), lambda j: (0, j, 0))]
        out_shape += [jax.ShapeDtypeStruct((1, d, hidden), BF16)] * n_in + [jax.ShapeDtypeStruct((1, hidden, d), BF16)]
        body = functools.partial(_gated_sample_kernel, n_in=n_in, mode=mode, n_steps=n_steps)
        sem = ("arbitrary",)
    else:
        assert seq_len % tm == 0 and width - 1 <= SUBLANES <= tm
        out_specs = [pl.BlockSpec((tm, d), row), pl.BlockSpec((None, SUBLANES, tf), lambda i, j: (i, 0, j))]
        out_shape = [jax.ShapeDtypeStruct((m, d), F32), jax.ShapeDtypeStruct((m // tm, SUBLANES, hidden), F32)]
        scratch += [pltpu.VMEM((SUBLANES + tm, tf), F32), pltpu.VMEM((tm, tf), F32), pltpu.VMEM((tm, tf), BF16),
                    pltpu.VMEM((nj, SUBLANES, tf), F32)]
        body = functools.partial(_gated_prompt_kernel, n_in=n_in, mode=mode, tiles_per_seq=seq_len // tm,
                                 row_chunk=min(GATE_ROW_CHUNK, tm))
        sem = ("arbitrary", "arbitrary")
    return pl.pallas_call(
        body, grid=grid, in_specs=in_specs, out_specs=out_specs, out_shape=out_shape,
        scratch_shapes=scratch, compiler_params=_params(*sem),
        name=f"gated_{mode}_{'sample' if sample else 'prompt'}",
    )(*args)


def _conformer_prompt_kernel(h_ref, hprev_ref, g_ref, wa_ref, wg_ref, bin_ref, wdw_ref, bdw_ref, lng_ref, lnb_ref, wo_ref, bo_ref,
                             ho_ref, ztail_ref, zext_ref, sh_ref, y_ref, s_ref, *, n_tiles, tiles_per_seq, row_chunk):
    i = pl.program_id(0)
    nb, tm, tn = y_ref.shape
    hist = zext_ref.shape[1] - tm
    width = wdw_ref.shape[0]
    base = hist - (width - 1)
    d = nb * tn
    cols = [slice(b * tn, (b + 1) * tn) for b in range(nb)]
    seq_start = jnp.minimum(i, n_tiles - 1) % tiles_per_seq == 0

    @pl.when(i == 0)
    def _():
        y_ref[...] = jnp.zeros(y_ref.shape, F32)

    @pl.when(seq_start)
    def _():
        for b in range(nb):
            zext_ref[b, 0:hist, :] = jnp.zeros((hist, tn), F32)

    @pl.when(jnp.logical_not(seq_start))
    def _():
        for b in range(nb):
            zext_ref[b, 0:hist, :] = zext_ref[b, tm:tm + hist, :]

    def in_proj_block(xn, b):
        a = _dot(xn, wa_ref[:, cols[b]]) + bin_ref[:, cols[b]]
        g = _dot(xn, wg_ref[:, cols[b]]) + bin_ref[:, d + b * tn:d + (b + 1) * tn]
        zext_ref[b, hist:hist + tm, :] = a * jax.nn.sigmoid(g)

    def conv_block(b):
        n_shifted = hist + tm - SUBLANES
        for s in range(1, SUBLANES):
            sh_ref[s - 1, 0:n_shifted, :] = zext_ref[b, s:s + n_shifted, :]
        lane_cols = [slice(c0, c0 + 128) for c0 in range(0, tn, 128)]
        for r0 in range(0, tm, row_chunk):
            acc = [None] * len(lane_cols)
            for k in range(width):
                q, s = divmod(base + k, SUBLANES)
                rows = slice(r0 + q * SUBLANES, r0 + q * SUBLANES + row_chunk)
                for c, cs in enumerate(lane_cols):
                    src = zext_ref[b, rows, cs] if s == 0 else sh_ref[s - 1, rows, cs]
                    term = wdw_ref[k:k + 1, b * tn + c * 128:b * tn + (c + 1) * 128] * src
                    acc[c] = term if acc[c] is None else acc[c] + term
            for c, cs in enumerate(lane_cols):
                y_ref[b, r0:r0 + row_chunk, cs] = acc[c]

    yb = [y_ref[b] + bdw_ref[:, cols[b]] for b in range(nb)]
    mean = sum(jnp.sum(v, axis=-1, keepdims=True) for v in yb) / d
    yc = [v - mean for v in yb]
    var = sum(jnp.sum(v * v, axis=-1, keepdims=True) for v in yc) / d
    rstd = lax.rsqrt(var + EPS)
    for b in range(nb):
        s_ref[b] = jax.nn.silu(yc[b] * rstd * lng_ref[:, cols[b]] + lnb_ref[:, cols[b]]).astype(BF16)

    xn = _rmsnorm(h_ref[...], g_ref[...]).astype(BF16)
    in_proj_block(xn, 0)
    for b in range(1, nb):
        conv_block(b - 1)
        in_proj_block(xn, b)
    conv_block(nb - 1)
    for b in range(nb):
        ztail_ref[b] = zext_ref[b, tm:tm + hist, :]

    for c in range(nb):
        acc = hprev_ref[:, cols[c]] + bo_ref[:, cols[c]]
        for b in range(nb):
            acc = acc + _dot(s_ref[b], wo_ref[b * tn:(b + 1) * tn, cols[c]])
        ho_ref[:, cols[c]] = acc


def _conformer_prompt(h, wts, w_in_parts, w_out, g_layer, layer, seq_len):
    m, d = h.shape
    width = wts["a_w_dw"].shape[1]
    tm, tn = min(CONFORMER_ROW_TILE, seq_len), min(COL_TILE, d)
    nb = d // tn
    n_tiles = m // tm
    hist = _round_up(width - 1, SUBLANES)
    assert seq_len % tm == 0 and hist <= tm
    vec = lambda i: (layer, 0, 0)
    tile = lambda i: (jnp.minimum(i, n_tiles - 1), 0)
    prev = lambda i: (jnp.maximum(i - 1, 0), 0)
    resident = pl.Buffered(1)
    return pl.pallas_call(
        functools.partial(_conformer_prompt_kernel, n_tiles=n_tiles, tiles_per_seq=seq_len // tm,
                          row_chunk=min(CONV_ROW_CHUNK, tm)),
        grid=(n_tiles + 1,),
        in_specs=[
            pl.BlockSpec((tm, d), tile),
            pl.BlockSpec((tm, d), prev),
            pl.BlockSpec((None, 1, d), lambda i: (g_layer, 0, 0)),
            pl.BlockSpec((None, d, d), lambda i: (w_in_parts[0][1], 0, 0), pipeline_mode=resident),
            pl.BlockSpec((None, d, d), lambda i: (w_in_parts[1][1], 0, 0), pipeline_mode=resident),
            pl.BlockSpec((None, 1, 2 * d), vec),
            pl.BlockSpec((None, width, d), vec),
            pl.BlockSpec((None, 1, d), vec),
            pl.BlockSpec((None, 1, d), vec),
            pl.BlockSpec((None, 1, d), vec),
            pl.BlockSpec((None, d, d), lambda i: (w_out[1], 0, 0), pipeline_mode=resident),
            pl.BlockSpec((None, 1, d), vec),
        ],
        out_specs=[
            pl.BlockSpec((tm, d), prev),
            pl.BlockSpec((None, nb, hist, tn), lambda i: (jnp.minimum(i, n_tiles - 1), 0, 0, 0)),
        ],
        out_shape=[jax.ShapeDtypeStruct((m, d), F32), jax.ShapeDtypeStruct((n_tiles, nb, hist, tn), F32)],
        scratch_shapes=[
            pltpu.VMEM((nb, hist + tm, tn), F32),
            pltpu.VMEM((SUBLANES - 1, hist + tm, tn), F32),
            pltpu.VMEM((nb, tm, tn), F32),
            pltpu.VMEM((nb, tm, tn), BF16),
        ],
        compiler_params=_params("arbitrary"),
        name="conformer_prompt",
    )(h, h, wts["g_mix"], w_in_parts[0][0], w_in_parts[1][0], wts["a_b_in"], wts["a_w_dw"], wts["a_b_dw"], wts["a_ln_g"],
      wts["a_ln_b"], w_out[0], wts["a_b_out"])


def _gmlp_prompt_kernel(h_ref, hcol_ref, g_ref, wu_ref, wv_ref, bu_ref, bv_ref, lng_ref, lnb_ref, ws_ref, bs_ref,
                        wo_ref, bo_ref, ho_ref, vlast_ref, xn_ref, u_ref, v_ref, s_ref, *, chunk, n_groups):
    j = pl.program_id(1)
    nb, tm, tn = u_ref.shape
    d = nb * tn
    gd = d // n_groups
    groups_per_block = tn // gd

    @pl.when(j == 0)
    def _():
        xn_ref[...] = _rmsnorm(h_ref[...], g_ref[...]).astype(BF16)

    @pl.when(j < nb)
    def _():
        xn = xn_ref[...]
        u_ref[j] = jax.nn.gelu(_dot(xn, wu_ref[...]) + bu_ref[...])
        v_ref[j] = jax.nn.gelu(_dot(xn, wv_ref[...]) + bv_ref[...])

    @pl.when(j == nb)
    def _():
        cols = [slice(b * tn, (b + 1) * tn) for b in range(nb)]
        mean = sum(jnp.sum(v_ref[b], axis=-1, keepdims=True) for b in range(nb)) / d
        var = sum(jnp.sum((v_ref[b] - mean) ** 2, axis=-1, keepdims=True) for b in range(nb)) / d
        rstd = lax.rsqrt(var + EPS)
        rows = lax.broadcasted_iota(jnp.int32, (chunk, chunk), 0)
        lanes = lax.broadcasted_iota(jnp.int32, (chunk, chunk), 1)
        tril = (rows >= lanes).astype(F32)
        for b in range(nb):
            vln = (v_ref[b] - mean) * rstd * lng_ref[:, cols[b]] + lnb_ref[:, cols[b]]
            vlast_ref[:, cols[b]] = vln[tm - chunk:, :]
            vb = vln.astype(BF16)
            for gl in range(groups_per_block):
                g = b * groups_per_block + gl
                wg = (ws_ref[g] * tril).astype(BF16)
                local, glob = slice(gl * gd, (gl + 1) * gd), slice(g * gd, (g + 1) * gd)
                for c in range(tm // chunk):
                    rs = slice(c * chunk, (c + 1) * chunk)
                    z = _dot(wg, vb[rs, local]) + bs_ref[:, glob]
                    s_ref[b, rs, local] = (u_ref[b, rs, local] * z).astype(BF16)

    @pl.when(j >= nb)
    def _():
        acc = hcol_ref[...] + bo_ref[...]
        for b in range(nb):
            acc = acc + _dot(s_ref[b], wo_ref[b * tn:(b + 1) * tn, :])
        ho_ref[...] = acc


def _gmlp_prompt(h, wts, w_in_parts, w_out, g_layer, layer, seq_len):
    m, d = h.shape
    n_groups, chunk = wts["c_w_s"].shape[1], wts["c_w_s"].shape[2]
    tm, tn = min(ROW_TILE, seq_len), min(COL_TILE, d)
    nb = d // tn
    n_tiles = m // tm
    tps = seq_len // tm
    assert seq_len % tm == 0 and tm % chunk == 0 and tn % (d // n_groups) == 0
    in_col = lambda i, j: jnp.where(j < nb, j, 0)
    out_col = lambda i, j: jnp.maximum(j - nb, 0)
    in_blk = lambda i, j: (layer, 0, in_col(i, j))
    in_blk_v = lambda i, j: (layer, 0, in_col(i, j) + nb)
    out_blk = lambda i, j: (layer, 0, out_col(i, j))
    vec = lambda i, j: (layer, 0, 0)
    return pl.pallas_call(
        functools.partial(_gmlp_prompt_kernel, chunk=chunk, n_groups=n_groups),
        grid=(n_tiles, 2 * nb),
        in_specs=[
            pl.BlockSpec((tm, d), lambda i, j: (jnp.minimum(jnp.where(j > 0, i + 1, i), n_tiles - 1), 0)),
            pl.BlockSpec((tm, tn), lambda i, j: (i, out_col(i, j))),
            pl.BlockSpec((None, 1, d), lambda i, j: (g_layer, 0, 0)),
            _weight_spec(w_in_parts[0], d, tn, in_col),
            _weight_spec(w_in_parts[1], d, tn, in_col),
            pl.BlockSpec((None, 1, tn), in_blk),
            pl.BlockSpec((None, 1, tn), in_blk_v),
            pl.BlockSpec((None, 1, d), vec),
            pl.BlockSpec((None, 1, d), vec),
            pl.BlockSpec((None, n_groups, chunk, chunk), lambda i, j: (layer, 0, 0, 0)),
            pl.BlockSpec((None, chunk, d), vec),
            _weight_spec(w_out, d, tn, out_col),
            pl.BlockSpec((None, 1, tn), out_blk),
        ],
        out_specs=[
            pl.BlockSpec((tm, tn), lambda i, j: (i, out_col(i, j))),
            pl.BlockSpec((None, chunk, d), lambda i, j: (i // tps, 0, 0)),
        ],
        out_shape=[jax.ShapeDtypeStruct((m, d), F32), jax.ShapeDtypeStruct((m // seq_len, chunk, d), F32)],
        scratch_shapes=[
            pltpu.VMEM((tm, d), BF16),
            pltpu.VMEM((nb, tm, tn), F32),
            pltpu.VMEM((nb, tm, tn), F32),
            pltpu.VMEM((nb, tm, tn), BF16),
        ],
        compiler_params=_params("arbitrary", "arbitrary"),
        name="gmlp_prompt",
    )(h, h, wts["g_mix"], w_in_parts[0][0], w_in_parts[1][0], wts["c_b_in"], wts["c_b_in"], wts["c_ln_g"], wts["c_ln_b"],
      wts["c_w_s"], wts["c_bs_rows"], w_out[0], wts["c_b_out"])


def _dwconv_sample_kernel(z_ref, st_ref, w_ref, y_ref, ns_ref, *, n_steps):
    n_hist = st_ref.shape[0]
    rb = z_ref.shape[0] // n_steps
    slabs = [st_ref[k] for k in range(n_hist)] + [z_ref[t * rb:(t + 1) * rb, :] for t in range(n_steps)]
    for k in range(n_hist):
        ns_ref[k] = slabs[n_steps + k]
    y = _conv_slabs(slabs, w_ref, n_steps)
    for t in range(n_steps):
        y_ref[t * rb:(t + 1) * rb, :] = y[t]


def _dwconv_sample(z, state, w, layer, n_steps):
    m, d = z.shape
    width = w.shape[1]
    batch = state.shape[2]
    tc = min(CONV_COL_TILE, d)
    return pl.pallas_call(
        functools.partial(_dwconv_sample_kernel, n_steps=n_steps),
        grid=(d // tc,),
        in_specs=[
            pl.BlockSpec((m, tc), lambda c: (0, c)),
            pl.BlockSpec((None, width - 1, batch, tc), lambda c: (layer, 0, 0, c)),
            pl.BlockSpec((None, width, tc), lambda c: (layer, 0, c)),
        ],
        out_specs=[pl.BlockSpec((m, tc), lambda c: (0, c)), pl.BlockSpec((width - 1, batch, tc), lambda c: (0, 0, c))],
        out_shape=[jax.ShapeDtypeStruct((m, d), F32), jax.ShapeDtypeStruct((width - 1, batch, d), F32)],
        compiler_params=_params("parallel"),
        name="dwconv_sample",
    )(z, state, w)


def _out_proj_kernel(h_ref, w_ref, b_ref, *rest, kind, n_steps, emit):
    j = pl.program_id(1)
    s_ref = rest[-1]
    copy_ref = rest[-2] if emit else None
    rest = rest[:-2] if emit else rest[:-1]
    if kind == "conf":
        y_ref, bdw_ref, lg_ref, lb_ref, o_ref = rest
    else:
        u_ref, v_ref, lg_ref, lb_ref, coef_ref, bs_ref, o_ref, vln_ref = rest

    @pl.when(j == 0)
    def _():
        if kind == "conf":
            s_ref[...] = jax.nn.silu(_layernorm(y_ref[...] + bdw_ref[...], lg_ref[...], lb_ref[...])).astype(BF16)
        else:
            rb = v_ref.shape[0] // n_steps
            vln = _layernorm(v_ref[...], lg_ref[...], lb_ref[...])
            vln_ref[...] = vln
            vb = vln.astype(BF16).astype(F32)
            for t in range(n_steps):
                z = bs_ref[t:t + 1, :]
                for s in range(t + 1):
                    coef = coef_ref[t * n_steps + s:t * n_steps + s + 1, :].astype(BF16).astype(F32)
                    z = z + coef * vb[s * rb:(s + 1) * rb, :]
                s_ref[t * rb:(t + 1) * rb, :] = (u_ref[t * rb:(t + 1) * rb, :] * z).astype(BF16)

    o_ref[...] = h_ref[...] + _dot(s_ref[...], _weight_tile(w_ref, copy_ref)) + b_ref[...]


def _out_proj(h, w, b, layer, kind, tm, ins, in_specs_extra, extra_out=None, *, emit, n_steps=0):
    m, d = h.shape
    tn = min(COL_TILE, d)
    col = lambda i, j: j
    in_specs = [
        pl.BlockSpec((tm, tn), lambda i, j: (i, j)),
        _weight_spec(w, d, tn, col),
        pl.BlockSpec((None, 1, tn), lambda i, j: (layer, 0, j)),
    ] + in_specs_extra
    out_specs = [pl.BlockSpec((tm, tn), lambda i, j: (i, j))]
    out_shape = [jax.ShapeDtypeStruct((m, d), F32)]
    if extra_out is not None:
        out_specs.append(extra_out[0])
        out_shape.append(extra_out[1])
    if emit:
        assert m == tm
        out_specs.append(_bf16_copy_spec(d, tn, col))
        out_shape.append(jax.ShapeDtypeStruct((1, d, d), BF16))
    return pl.pallas_call(
        functools.partial(_out_proj_kernel, kind=kind, n_steps=n_steps, emit=emit),
        grid=(m // tm, d // tn), in_specs=in_specs, out_specs=out_specs, out_shape=out_shape,
        scratch_shapes=[pltpu.VMEM((tm, d), BF16)],
        compiler_params=_params("arbitrary", "arbitrary"),
        name=f"out_proj_{kind}",
    )(h, w[0], b, *ins)


def _ple_kernel(h_ref, g_ref, wg_ref, p_ref, wp_ref, *rest, final, tn):
    if final:
        gf_ref, o_ref, y_ref = rest
    else:
        (o_ref,) = rest
    d = h_ref.shape[1]
    xn = _rmsnorm(h_ref[...], g_ref[...]).astype(BF16)
    pb = p_ref[...].astype(BF16)
    for c0 in range(0, d, tn):
        gate = jax.nn.sigmoid(_dot(xn, wg_ref[:, c0:c0 + tn]))
        o_ref[:, c0:c0 + tn] = h_ref[:, c0:c0 + tn] + gate * _dot(pb, wp_ref[:, c0:c0 + tn])
    if final:
        y_ref[...] = _rmsnorm(o_ref[...], gf_ref[...])


def _ple(h, g, wg, p, wp, layer, g_final=None):
    m, d = h.shape
    pdim = p.shape[2]
    tm = min(ROW_TILE, m)
    final = g_final is not None
    whole = lambda i: 0
    in_specs = [
        pl.BlockSpec((tm, d), lambda i: (i, 0)),
        pl.BlockSpec((None, 1, d), lambda i: (layer, 0, 0)),
        _weight_spec(wg, d, d, whole),
        pl.BlockSpec((None, tm, pdim), lambda i: (layer, i, 0)),
        _weight_spec(wp, pdim, d, whole),
    ]
    args = [h, g, wg[0], p, wp[0]]
    out_specs = [pl.BlockSpec((tm, d), lambda i: (i, 0))]
    out_shape = [jax.ShapeDtypeStruct((m, d), F32)]
    if final:
        in_specs.append(pl.BlockSpec((1, d), lambda i: (0, 0)))
        args.append(g_final)
        out_specs.append(pl.BlockSpec((tm, d), lambda i: (i, 0)))
        out_shape.append(jax.ShapeDtypeStruct((m, d), F32))
    return pl.pallas_call(
        functools.partial(_ple_kernel, final=final, tn=min(COL_TILE, d)),
        grid=(m // tm,), in_specs=in_specs, out_specs=out_specs, out_shape=out_shape,
        compiler_params=_params("parallel"),
        name="ple_final" if final else "ple",
    )(*args)


def _ple_sample_kernel(h_ref, hcol_ref, g_ref, wg_ref, p_ref, wp_ref, o_ref, wg_copy_ref, wp_copy_ref, xn_ref):
    @pl.when(pl.program_id(0) == 0)
    def _():
        xn_ref[...] = _rmsnorm(h_ref[...], g_ref[...]).astype(BF16)

    gate = jax.nn.sigmoid(_dot(xn_ref[...], _weight_tile(wg_ref, wg_copy_ref)))
    o_ref[...] = hcol_ref[...] + gate * _dot(p_ref[...].astype(BF16), _weight_tile(wp_ref, wp_copy_ref))


def _ple_sample(h, g, wg, p, wp, layer):
    m, d = h.shape
    pdim = p.shape[2]
    tn = min(COL_TILE, d)
    col = lambda j: j
    return pl.pallas_call(
        _ple_sample_kernel,
        grid=(d // tn,),
        in_specs=[
            pl.BlockSpec((m, d), lambda j: (0, 0)),
            pl.BlockSpec((m, tn), lambda j: (0, j)),
            pl.BlockSpec((None, 1, d), lambda j: (layer, 0, 0)),
            _weight_spec(wg, d, tn, col),
            pl.BlockSpec((None, m, pdim), lambda j: (layer, 0, 0)),
            _weight_spec(wp, pdim, tn, col),
        ],
        out_specs=[pl.BlockSpec((m, tn), lambda j: (0, j)), _bf16_copy_spec(d, tn, col), _bf16_copy_spec(pdim, tn, col)],
        out_shape=[jax.ShapeDtypeStruct((m, d), F32), jax.ShapeDtypeStruct((1, d, d), BF16),
                   jax.ShapeDtypeStruct((1, pdim, d), BF16)],
        scratch_shapes=[pltpu.VMEM((m, d), BF16)],
        compiler_params=_params("arbitrary"),
        name="ple_sample",
    )(h, h, g, wg[0], p, wp[0])


def _final_norm_kernel(h_ref, g_ref, y_ref):
    y_ref[...] = _rmsnorm(h_ref[...], g_ref[...])


def _final_norm(h, g):
    m, d = h.shape
    tm = min(ROW_TILE, m)
    return pl.pallas_call(
        _final_norm_kernel,
        grid=(m // tm,),
        in_specs=[pl.BlockSpec((tm, d), lambda i: (i, 0)), pl.BlockSpec((1, d), lambda i: (0, 0))],
        out_specs=pl.BlockSpec((tm, d), lambda i: (i, 0)),
        out_shape=jax.ShapeDtypeStruct((m, d), F32),
        compiler_params=_params("parallel"),
        name="final_norm",
    )(h, g)


def _trunk(h, p, wts, weight, *, seq_len, n_steps, states):
    sample = states is not None
    depth = wts["g_mix"].shape[0]
    new_a, new_b, new_c, new_f, copies = [], [], [], [], {}
    for i in range(depth):
        kind, l = i % 3, i // 3
        m, d = h.shape
        vec = lambda i_, j_, l=l: (l, 0, 0)
        if kind == 0 and not sample:
            h, ns = _conformer_prompt(h, wts, [weight("a_w_in", l, 0, 2), weight("a_w_in", l, 1, 2)],
                                      weight("a_w_out", l), i, l, seq_len)
            new_a.append(ns)
        elif kind == 0:
            z, wa_copy, wg_copy = _in_proj(h, wts["g_mix"], i, [weight("a_w_in", l, 0, 2), weight("a_w_in", l, 1, 2)],
                                           wts["a_b_in"], l, "glu", True)
            conv, ns = _dwconv_sample(z, states[0], wts["a_w_dw"], l, n_steps)
            new_a.append(ns)
            h, wo_copy = _out_proj(
                h, weight("a_w_out", l), wts["a_b_out"], l, "conf", m,
                [conv, wts["a_b_dw"], wts["a_ln_g"], wts["a_ln_b"]],
                [pl.BlockSpec((m, d), lambda i_, j_: (0, 0))] + [pl.BlockSpec((None, 1, d), vec)] * 3, emit=True)
            copies["a_w_in", l], copies["a_w_out", l] = (wa_copy, wg_copy), (wo_copy,)
        elif kind == 1:
            res = _gated(h, wts["g_mix"], i, [weight("b_w_in", l, part, 3) for part in range(3)],
                         wts["b_w_dw"], wts["b_zero_bias"], weight("b_w_out", l), l, "short",
                         seq_len=seq_len, state=states[1] if sample else None, n_steps=n_steps)
            h, ns = res[:2]
            new_b.append(ns)
            if sample:
                copies["b_w_in", l], copies["b_w_out", l] = tuple(res[2:5]), (res[5],)
        elif not sample:
            h, nv = _gmlp_prompt(h, wts, [weight("c_w_in", l, 0, 2), weight("c_w_in", l, 1, 2)],
                                 weight("c_w_out", l), i, l, seq_len)
            new_c.append(nv)
        else:
            u, v, wu_copy, wv_copy = _in_proj(h, wts["g_mix"], i, [weight("c_w_in", l, 0, 2), weight("c_w_in", l, 1, 2)],
                                              wts["c_b_in"], l, "gelu2", True)
            full = pl.BlockSpec((m, d), lambda i_, j_: (0, 0))
            h, nv, wo_copy = _out_proj(
                h, weight("c_w_out", l), wts["c_b_out"], l, "gmlp_sample", m,
                [u, v, wts["c_ln_g"], wts["c_ln_b"], wts["c_coef"], wts["c_bs_steps"]],
                [full, full] + [pl.BlockSpec((None, 1, d), vec)] * 2
                + [pl.BlockSpec((None, n_steps * n_steps, d), vec), pl.BlockSpec((None, n_steps, d), vec)],
                (full, jax.ShapeDtypeStruct((m, d), F32)), emit=True, n_steps=n_steps)
            copies["c_w_in", l], copies["c_w_out", l] = (wu_copy, wv_copy), (wo_copy,)
            new_c.append(nv)
        res = _gated(h, wts["g_ffn"], i, [weight("f_w_gate", i), weight("f_w_up", i)],
                     wts["f_w_dw"], wts["f_b_dw"], weight("f_w_down", i), i, "ffn",
                     seq_len=seq_len, state=states[2] if sample else None, n_steps=n_steps)
        h, nf = res[:2]
        new_f.append(nf)
        if sample:
            copies["f_w_gate", i], copies["f_w_up", i], copies["f_w_down", i] = (res[2],), (res[3],), (res[4],)
            h, wg_copy, wp_copy = _ple_sample(h, wts["g_ple"], weight("ple_w_gate", i), p, weight("ple_w_proj", i), i)
            copies["ple_w_gate", i], copies["ple_w_proj", i] = (wg_copy,), (wp_copy,)
            if i == depth - 1:
                y = _final_norm(h, wts["g_final"])
        elif i == depth - 1:
            h, y = _ple(h, wts["g_ple"], weight("ple_w_gate", i), p, weight("ple_w_proj", i), i, wts["g_final"])
        else:
            (h,) = _ple(h, wts["g_ple"], weight("ple_w_gate", i), p, weight("ple_w_proj", i), i)
    return y, new_a, new_b, new_c, new_f, copies


def kernel(x_prompt, x_sample, p_prompt, p_sample, state_conformer, state_shortconv, state_ffn, g_mix, g_ffn, g_ple, g_final, a_w_in, a_b_in, a_w_dw, a_b_dw, a_ln_g, a_ln_b, a_w_out, a_b_out, b_w_in, b_w_dw, b_w_out, c_w_in, c_b_in, c_ln_g, c_ln_b, c_w_s, c_b_s, c_w_out, c_b_out, f_w_gate, f_w_dw, f_b_dw, f_w_up, f_w_down, ple_w_proj, ple_w_gate):
    bp, seq_len, d = x_prompt.shape
    bs, n_steps, _ = x_sample.shape
    depth = g_mix.shape[0]
    n_groups, chunk = c_w_s.shape[1], c_w_s.shape[2]
    gd = d // n_groups
    assert n_steps <= chunk

    row3 = lambda a: a[:, None, :]
    wts = dict(
        g_mix=row3(g_mix), g_ffn=row3(g_ffn), g_ple=row3(g_ple), g_final=g_final[None, :],
        a_b_in=row3(a_b_in), a_w_dw=a_w_dw, a_b_dw=row3(a_b_dw),
        a_ln_g=row3(a_ln_g), a_ln_b=row3(a_ln_b), a_b_out=row3(a_b_out),
        b_w_dw=b_w_dw, b_zero_bias=jnp.zeros((b_w_dw.shape[0], 1, d), F32),
        c_b_in=row3(c_b_in), c_ln_g=row3(c_ln_g), c_ln_b=row3(c_ln_b), c_w_s=c_w_s, c_b_out=row3(c_b_out),
        c_bs_rows=jnp.repeat(jnp.swapaxes(c_b_s, 1, 2), gd, axis=2),
        c_bs_steps=jnp.repeat(jnp.swapaxes(c_b_s, 1, 2)[:, :n_steps], gd, axis=2),
        c_coef=jnp.repeat(jnp.transpose(c_w_s[:, :, :n_steps, :n_steps], (0, 2, 3, 1)).reshape(
            c_w_s.shape[0], n_steps * n_steps, n_groups), gd, axis=2),
        f_w_dw=f_w_dw, f_b_dw=row3(f_b_dw),
    )
    big = dict(a_w_in=a_w_in, a_w_out=a_w_out, b_w_in=b_w_in, b_w_out=b_w_out, c_w_in=c_w_in, c_w_out=c_w_out,
               f_w_gate=f_w_gate, f_w_up=f_w_up, f_w_down=f_w_down, ple_w_gate=ple_w_gate, ple_w_proj=ple_w_proj)

    tmaj = lambda a: jnp.swapaxes(a, 1, 2)
    hs = jnp.swapaxes(x_sample, 0, 1).reshape(n_steps * bs, d)
    ps = tmaj(p_sample).reshape(depth, n_steps * bs, -1)
    states = (tmaj(state_conformer), tmaj(state_shortconv), tmaj(state_ffn))
    ys, a_s, b_s, c_s, f_s, copies = _trunk(
        hs, ps, wts, lambda name, layer, part=0, n_parts=1: (big[name], layer, part, n_parts),
        seq_len=None, n_steps=n_steps, states=states)

    hp = x_prompt.reshape(bp * seq_len, d)
    pp = p_prompt.reshape(depth, bp * seq_len, -1)
    yp, a_p, b_p, c_p, f_p, _ = _trunk(
        hp, pp, wts, lambda name, layer, part=0, n_parts=1: (copies[name, layer][part], 0, 0, 1),
        seq_len=seq_len, n_steps=None, states=None)

    tm = min(ROW_TILE, seq_len)
    tps = seq_len // tm

    def prompt_tail(tails, width):
        per_seq = tails.shape[0] // bp
        return tails[per_seq - 1::per_seq, SUBLANES - (width - 1):, :]

    def conformer_tail(tails):
        per_seq = tails.shape[0] // bp
        last = tails[per_seq - 1::per_seq]
        rows = jnp.swapaxes(last, 1, 2).reshape(bp, last.shape[2], d)
        return rows[:, rows.shape[1] - (wa - 1):]

    bmaj = lambda a: jnp.swapaxes(a, 0, 1)
    wa, wb, wf = a_w_dw.shape[1], b_w_dw.shape[1], f_w_dw.shape[1]
    return (
        yp.reshape(bp, seq_len, d),
        bmaj(ys.reshape(n_steps, bs, d)),
        jnp.stack([conformer_tail(t) for t in a_p]),
        jnp.stack([bmaj(s) for s in a_s]),
        jnp.stack([prompt_tail(t, wb) for t in b_p]),
        jnp.stack([bmaj(s) for s in b_s]),
        jnp.stack(c_p),
        jnp.stack([bmaj(v.reshape(n_steps, bs, d)) for v in c_s]),
        jnp.stack([prompt_tail(t, wf) for t in f_p]),
        jnp.stack([bmaj(s) for s in f_s]),
    )
```

```python
import functools

import jax
import jax.numpy as jnp
from jax import lax
from jax.experimental import pallas as pl
from jax.experimental.pallas import tpu as pltpu

EPS = 1e-6
F32 = jnp.float32
BF16 = jnp.bfloat16

ROW_TILE = 512
PROJ_ROW_TILE = 1024
COL_TILE = 512
CONFORMER_ROW_TILE = 256
GMLP_ROW_TILE = 256
CONV_COL_TILE = 256
SAMPLE_COL_TILE = 256
CONV_ROW_CHUNK = 64
GATE_ROW_CHUNK = 64
SUBLANES = 8
VMEM_LIMIT = 56 * 1024 * 1024


def _params(*sem):
    return pltpu.CompilerParams(dimension_semantics=sem, vmem_limit_bytes=VMEM_LIMIT)


def _rmsnorm(x, g):
    return x * lax.rsqrt(jnp.mean(x * x, axis=-1, keepdims=True) + EPS) * g


def _layernorm(x, g, b):
    xc = x - jnp.mean(x, axis=-1, keepdims=True)
    return xc * lax.rsqrt(jnp.mean(xc * xc, axis=-1, keepdims=True) + EPS) * g + b


def _dot(a, b):
    return jnp.dot(a, b, preferred_element_type=F32)


def _round_up(n, m):
    return -(-n // m) * m


def _weight_spec(operand, k, tn, col_of):
    arr, layer, part, n_parts = operand
    off = part * (arr.shape[2] // n_parts // tn)
    return pl.BlockSpec((None, k, tn), lambda *idx: (layer, 0, col_of(*idx) + off))


def _weight_row_spec(operand, tk, n):
    _, layer, _, n_parts = operand
    assert n_parts == 1
    return pl.BlockSpec((None, tk, n), lambda *idx: (layer, idx[-1], 0))


def _bf16_copy_spec(k, tn, col_of):
    return pl.BlockSpec((None, k, tn), lambda *idx: (0, 0, col_of(*idx)))


def _weight_tile(w_ref, copy_ref):
    if copy_ref is None:
        return w_ref[...]
    wb = w_ref[...].astype(BF16)
    copy_ref[...] = wb
    return wb


def _conv_slabs(slabs, w_ref, n_out):
    width = w_ref.shape[0]
    out = []
    for t in range(n_out):
        acc = None
        for k in range(width):
            term = w_ref[k:k + 1, :] * slabs[t + k]
            acc = term if acc is None else acc + term
        out.append(acc)
    return out


def _in_proj_kernel(h_ref, g_ref, w0_ref, w1_ref, b0_ref, b1_ref, *rest, kind, emit):
    outs, xn_ref = rest[:-1], rest[-1]

    @pl.when(pl.program_id(1) == 0)
    def _():
        xn_ref[...] = _rmsnorm(h_ref[...], g_ref[...]).astype(BF16)

    n_act = 1 if kind == "glu" else 2
    xn = xn_ref[...]
    w0 = _weight_tile(w0_ref, outs[n_act] if emit else None)
    w1 = _weight_tile(w1_ref, outs[n_act + 1] if emit else None)
    a = _dot(xn, w0) + b0_ref[...]
    b = _dot(xn, w1) + b1_ref[...]
    if kind == "glu":
        outs[0][...] = a * jax.nn.sigmoid(b)
    else:
        outs[0][...] = jax.nn.gelu(a)
        outs[1][...] = jax.nn.gelu(b)


def _in_proj(h, g, g_layer, w_parts, b, layer, kind, emit):
    m, d = h.shape
    tm, tn = min(PROJ_ROW_TILE, m), min(COL_TILE, d)
    nj = d // tn
    n_act = 1 if kind == "glu" else 2
    col = lambda i, j: j
    out_specs = [pl.BlockSpec((tm, tn), lambda i, j: (i, j))] * n_act
    out_shape = [jax.ShapeDtypeStruct((m, d), F32)] * n_act
    if emit:
        assert m == tm
        out_specs += [_bf16_copy_spec(d, tn, col)] * 2
        out_shape += [jax.ShapeDtypeStruct((1, d, d), BF16)] * 2
    return pl.pallas_call(
        functools.partial(_in_proj_kernel, kind=kind, emit=emit),
        grid=(m // tm, nj),
        in_specs=[
            pl.BlockSpec((tm, d), lambda i, j: (i, 0)),
            pl.BlockSpec((None, 1, d), lambda i, j: (g_layer, 0, 0)),
            _weight_spec(w_parts[0], d, tn, col),
            _weight_spec(w_parts[1], d, tn, col),
            pl.BlockSpec((None, 1, tn), lambda i, j: (layer, 0, j)),
            pl.BlockSpec((None, 1, tn), lambda i, j: (layer, 0, j + nj)),
        ],
        out_specs=out_specs, out_shape=out_shape,
        scratch_shapes=[pltpu.VMEM((tm, d), BF16)],
        compiler_params=_params("parallel", "arbitrary"),
        name=f"in_proj_{kind}",
    )(h, g, w_parts[0][0], w_parts[1][0], b, b)


def _gate(conv, bias, other, mode):
    if mode == "ffn":
        return jax.nn.silu(conv + bias) * other
    return other * conv


def _gated_prompt_kernel(*refs, n_in, mode, tiles_per_seq, row_chunk):
    h_ref, g_ref = refs[:2]
    w_in = refs[2:2 + n_in]
    wdw_ref, bdw_ref, w2_ref, ho_ref, tail_ref, xn_ref, ext_ref, oth_ref, hid_ref, carry_ref = refs[2 + n_in:]
    i, j = pl.program_id(0), pl.program_id(1)
    tm, tf = oth_ref.shape
    hist = SUBLANES
    width = wdw_ref.shape[0]
    base = hist - (width - 1)

    @pl.when(j == 0)
    def _():
        xn_ref[...] = _rmsnorm(h_ref[...], g_ref[...]).astype(BF16)
        ho_ref[...] = h_ref[...]

        @pl.when(i % tiles_per_seq == 0)
        def _():
            carry_ref[...] = jnp.zeros(carry_ref.shape, F32)

    xn = xn_ref[...]
    ext_ref[0:hist, :] = carry_ref[j]
    if mode == "ffn":
        ext_ref[hist:hist + tm, :] = _dot(xn, w_in[0][...])
        oth_ref[...] = _dot(xn, w_in[1][...])
    else:
        oth_ref[...] = _dot(xn, w_in[0][...])
        ext_ref[hist:hist + tm, :] = _dot(xn, w_in[1][...]) * _dot(xn, w_in[2][...])
    tail = ext_ref[tm:tm + hist, :]
    carry_ref[j] = tail
    tail_ref[...] = tail

    for r0 in range(0, tm, row_chunk):
        for c0 in range(0, tf, 128):
            cs = slice(c0, c0 + 128)
            conv = None
            for k in range(width):
                term = wdw_ref[k:k + 1, cs] * ext_ref[base + k + r0:base + k + r0 + row_chunk, cs]
                conv = term if conv is None else conv + term
            hid = _gate(conv, bdw_ref[:, cs], oth_ref[r0:r0 + row_chunk, cs], mode)
            hid_ref[r0:r0 + row_chunk, cs] = hid.astype(BF16)

    ho_ref[...] += _dot(hid_ref[...], w2_ref[...])


def _gated_sample_kernel(*refs, n_in, mode, n_steps):
    h_ref, g_ref = refs[:2]
    w_in = refs[2:2 + n_in]
    wdw_ref, bdw_ref, w2_ref, st_ref, ho_ref, ns_ref = refs[2 + n_in:8 + n_in]
    copies = refs[8 + n_in:9 + 2 * n_in]
    xn_ref = refs[-1]
    j = pl.program_id(0)
    n_hist = st_ref.shape[0]
    rb = h_ref.shape[0] // n_steps

    @pl.when(j == 0)
    def _():
        xn_ref[...] = _rmsnorm(h_ref[...], g_ref[...]).astype(BF16)
        ho_ref[...] = h_ref[...]

    xn = xn_ref[...]
    proj = [_dot(xn, _weight_tile(w, c)) for w, c in zip(w_in, copies[:n_in])]
    v, other = (proj[0], proj[1]) if mode == "ffn" else (proj[1] * proj[2], proj[0])
    slabs = [st_ref[k] for k in range(n_hist)] + [v[t * rb:(t + 1) * rb, :] for t in range(n_steps)]
    for k in range(n_hist):
        ns_ref[k] = slabs[n_steps + k]
    conv = jnp.concatenate(_conv_slabs(slabs, wdw_ref, n_steps), axis=0)
    hid = _gate(conv, bdw_ref[...], other, mode)
    ho_ref[...] += _dot(hid.astype(BF16), _weight_tile(w2_ref, copies[n_in]))


def _gated(h, g, g_layer, w_in_list, wdw, bdw, w2, layer, mode, *, seq_len=None, state=None, n_steps=None):
    m, d = h.shape
    hidden = wdw.shape[2]
    sample = state is not None
    tf = min(SAMPLE_COL_TILE if sample else COL_TILE, hidden)
    nj = hidden // tf
    width = wdw.shape[1]
    n_in = len(w_in_list)
    tm = m if sample else min(ROW_TILE, seq_len)
    if sample:
        row = lambda j: (0, 0)
        vec = lambda l: (lambda j: (l, 0, 0))
        hid_vec = lambda j: (layer, 0, j)
        grid = (nj,)
    else:
        row = lambda i, j: (i, 0)
        vec = lambda l: (lambda i, j: (l, 0, 0))
        hid_vec = lambda i, j: (layer, 0, j)
        grid = (m // tm, nj)
    col = lambda *idx: idx[-1]
    in_specs = [pl.BlockSpec((tm, d), row), pl.BlockSpec((None, 1, d), vec(g_layer))]
    in_specs += [_weight_spec(w, d, tf, col) for w in w_in_list]
    in_specs += [
        pl.BlockSpec((None, width, tf), hid_vec),
        pl.BlockSpec((None, 1, tf), hid_vec),
        _weight_row_spec(w2, tf, d),
    ]
    args = [h, g] + [w[0] for w in w_in_list] + [wdw, bdw, w2[0]]
    scratch = [pltpu.VMEM((tm, d), BF16)]
    if sample:
        batch = state.shape[2]
        in_specs.append(pl.BlockSpec((None, width - 1, batch, tf), lambda j: (layer, 0, 0, j)))
        args.append(state)
        out_specs = [pl.BlockSpec((tm, d), row), pl.BlockSpec((width - 1, batch, tf), lambda j: (0, 0, j))]
        out_shape = [jax.ShapeDtypeStruct((m, d), F32), jax.ShapeDtypeStruct((width - 1, batch, hidden), F32)]
        out_specs += [_bf16_copy_spec(d, tf, col)] * n_in + [pl.BlockSpec((None, tf, d), lambda j: (0, j, 0))]
        out_shape += [jax.ShapeDtypeStruct((1, d, hidden), BF16)] * n_in + [jax.ShapeDtypeStruct((1, hidden, d), BF16)]
        body = functools.partial(_gated_sample_kernel, n_in=n_in, mode=mode, n_steps=n_steps)
        sem = ("arbitrary",)
    else:
        assert seq_len % tm == 0 and width - 1 <= SUBLANES <= tm
        out_specs = [pl.BlockSpec((tm, d), row), pl.BlockSpec((None, SUBLANES, tf), lambda i, j: (i, 0, j))]
        out_shape = [jax.ShapeDtypeStruct((m, d), F32), jax.ShapeDtypeStruct((m // tm, SUBLANES, hidden), F32)]
        scratch += [pltpu.VMEM((SUBLANES + tm, tf), F32), pltpu.VMEM((tm, tf), F32), pltpu.VMEM((tm, tf), BF16),
                    pltpu.VMEM((nj, SUBLANES, tf), F32)]
        body = functools.partial(_gated_prompt_kernel, n_in=n_in, mode=mode, tiles_per_seq=seq_len // tm,
                                 row_chunk=min(GATE_ROW_CHUNK, tm))
        sem = ("arbitrary", "arbitrary")
    return pl.pallas_call(
        body, grid=grid, in_specs=in_specs, out_specs=out_specs, out_shape=out_shape,
        scratch_shapes=scratch, compiler_params=_params(*sem),
        name=f"gated_{mode}_{'sample' if sample else 'prompt'}",
    )(*args)


def _conformer_prompt_kernel(h_ref, hprev_ref, g_ref, wa_ref, wg_ref, bin_ref, wdw_ref, bdw_ref, lng_ref, lnb_ref, wo_ref, bo_ref,
                             ho_ref, ztail_ref, zext_ref, sh_ref, y_ref, s_ref, *, n_tiles, tiles_per_seq, row_chunk):
    i = pl.program_id(0)
    nb, tm, tn = y_ref.shape
    hist = zext_ref.shape[1] - tm
    width = wdw_ref.shape[0]
    base = hist - (width - 1)
    d = nb * tn
    cols = [slice(b * tn, (b + 1) * tn) for b in range(nb)]
    seq_start = jnp.minimum(i, n_tiles - 1) % tiles_per_seq == 0

    @pl.when(i == 0)
    def _():
        y_ref[...] = jnp.zeros(y_ref.shape, F32)

    @pl.when(seq_start)
    def _():
        for b in range(nb):
            zext_ref[b, 0:hist, :] = jnp.zeros((hist, tn), F32)

    @pl.when(jnp.logical_not(seq_start))
    def _():
        for b in range(nb):
            zext_ref[b, 0:hist, :] = zext_ref[b, tm:tm + hist, :]

    def in_proj_block(xn, b):
        a = _dot(xn, wa_ref[:, cols[b]]) + bin_ref[:, cols[b]]
        g = _dot(xn, wg_ref[:, cols[b]]) + bin_ref[:, d + b * tn:d + (b + 1) * tn]
        zext_ref[b, hist:hist + tm, :] = a * jax.nn.sigmoid(g)

    def conv_block(b):
        n_shifted = hist + tm - SUBLANES
        for s in range(1, SUBLANES):
            sh_ref[s - 1, 0:n_shifted, :] = zext_ref[b, s:s + n_shifted, :]
        for c0 in range(0, tn, 128):
            cs = slice(c0, c0 + 128)
            w_rows = [wdw_ref[k:k + 1, b * tn + c0:b * tn + c0 + 128] for k in range(width)]
            for r0 in range(0, tm, row_chunk):
                acc = None
                for k in range(width):
                    q, s = divmod(base + k, SUBLANES)
                    rows = slice(r0 + q * SUBLANES, r0 + q * SUBLANES + row_chunk)
                    src = zext_ref[b, rows, cs] if s == 0 else sh_ref[s - 1, rows, cs]
                    term = w_rows[k] * src
                    acc = term if acc is None else acc + term
                y_ref[b, r0:r0 + row_chunk, cs] = acc

    yb = [y_ref[b] + bdw_ref[:, cols[b]] for b in range(nb)]
    mean = sum(jnp.sum(v, axis=-1, keepdims=True) for v in yb) / d
    yc = [v - mean for v in yb]
    var = sum(jnp.sum(v * v, axis=-1, keepdims=True) for v in yc) / d
    rstd = lax.rsqrt(var + EPS)
    for b in range(nb):
        s_ref[b] = jax.nn.silu(yc[b] * rstd * lng_ref[:, cols[b]] + lnb_ref[:, cols[b]]).astype(BF16)

    xn = _rmsnorm(h_ref[...], g_ref[...]).astype(BF16)
    in_proj_block(xn, 0)
    for b in range(1, nb):
        conv_block(b - 1)
        in_proj_block(xn, b)
    conv_block(nb - 1)
    for b in range(nb):
        ztail_ref[b] = zext_ref[b, tm:tm + hist, :]

    for c in range(nb):
        acc = hprev_ref[:, cols[c]] + bo_ref[:, cols[c]]
        for b in range(nb):
            acc = acc + _dot(s_ref[b], wo_ref[b * tn:(b + 1) * tn, cols[c]])
        ho_ref[:, cols[c]] = acc


def _conformer_prompt(h, wts, w_in_parts, w_out, g_layer, layer, seq_len):
    m, d = h.shape
    width = wts["a_w_dw"].shape[1]
    tm, tn = min(CONFORMER_ROW_TILE, seq_len), min(COL_TILE, d)
    nb = d // tn
    n_tiles = m // tm
    hist = _round_up(width - 1, SUBLANES)
    assert seq_len % tm == 0 and hist <= tm
    vec = lambda i: (layer, 0, 0)
    tile = lambda i: (jnp.minimum(i, n_tiles - 1), 0)
    prev = lambda i: (jnp.maximum(i - 1, 0), 0)
    resident = pl.Buffered(1)
    return pl.pallas_call(
        functools.partial(_conformer_prompt_kernel, n_tiles=n_tiles, tiles_per_seq=seq_len // tm,
                          row_chunk=min(CONV_ROW_CHUNK, tm)),
        grid=(n_tiles + 1,),
        in_specs=[
            pl.BlockSpec((tm, d), tile),
            pl.BlockSpec((tm, d), prev),
            pl.BlockSpec((None, 1, d), lambda i: (g_layer, 0, 0)),
            pl.BlockSpec((None, d, d), lambda i: (w_in_parts[0][1], 0, 0), pipeline_mode=resident),
            pl.BlockSpec((None, d, d), lambda i: (w_in_parts[1][1], 0, 0), pipeline_mode=resident),
            pl.BlockSpec((None, 1, 2 * d), vec),
            pl.BlockSpec((None, width, d), vec),
            pl.BlockSpec((None, 1, d), vec),
            pl.BlockSpec((None, 1, d), vec),
            pl.BlockSpec((None, 1, d), vec),
            pl.BlockSpec((None, d, d), lambda i: (w_out[1], 0, 0), pipeline_mode=resident),
            pl.BlockSpec((None, 1, d), vec),
        ],
        out_specs=[
            pl.BlockSpec((tm, d), prev),
            pl.BlockSpec((None, nb, hist, tn), lambda i: (jnp.minimum(i, n_tiles - 1), 0, 0, 0)),
        ],
        out_shape=[jax.ShapeDtypeStruct((m, d), F32), jax.ShapeDtypeStruct((n_tiles, nb, hist, tn), F32)],
        scratch_shapes=[
            pltpu.VMEM((nb, hist + tm, tn), F32),
            pltpu.VMEM((SUBLANES - 1, hist + tm, tn), F32),
            pltpu.VMEM((nb, tm, tn), F32),
            pltpu.VMEM((nb, tm, tn), BF16),
        ],
        compiler_params=_params("arbitrary"),
        name="conformer_prompt",
    )(h, h, wts["g_mix"], w_in_parts[0][0], w_in_parts[1][0], wts["a_b_in"], wts["a_w_dw"], wts["a_b_dw"], wts["a_ln_g"],
      wts["a_ln_b"], w_out[0], wts["a_b_out"])


def _gmlp_prompt_kernel(h_ref, g_ref, wu_ref, wv_ref, bin_ref, lng_ref, lnb_ref, ws_ref, bs_ref, wo_ref, bo_ref,
                        ho_ref, vlast_ref, u_ref, v_ref, s_ref, *, chunk, n_groups):
    nb, tm, tn = u_ref.shape
    d = nb * tn
    gd = d // n_groups
    groups_per_block = tn // gd
    cols = [slice(b * tn, (b + 1) * tn) for b in range(nb)]

    xn = _rmsnorm(h_ref[...], g_ref[...]).astype(BF16)
    for b in range(nb):
        u_ref[b] = jax.nn.gelu(_dot(xn, wu_ref[:, cols[b]]) + bin_ref[:, cols[b]])
        v_ref[b] = jax.nn.gelu(_dot(xn, wv_ref[:, cols[b]]) + bin_ref[:, d + b * tn:d + (b + 1) * tn])

    mean = sum(jnp.sum(v_ref[b], axis=-1, keepdims=True) for b in range(nb)) / d
    var = sum(jnp.sum((v_ref[b] - mean) ** 2, axis=-1, keepdims=True) for b in range(nb)) / d
    rstd = lax.rsqrt(var + EPS)
    rows = lax.broadcasted_iota(jnp.int32, (chunk, chunk), 0)
    lanes = lax.broadcasted_iota(jnp.int32, (chunk, chunk), 1)
    tril = (rows >= lanes).astype(F32)
    for b in range(nb):
        vln = (v_ref[b] - mean) * rstd * lng_ref[:, cols[b]] + lnb_ref[:, cols[b]]
        vlast_ref[:, cols[b]] = vln[tm - chunk:, :]
        vb = vln.astype(BF16)
        for gl in range(groups_per_block):
            g = b * groups_per_block + gl
            wg = (ws_ref[g] * tril).astype(BF16)
            local, glob = slice(gl * gd, (gl + 1) * gd), slice(g * gd, (g + 1) * gd)
            for c in range(tm // chunk):
                rs = slice(c * chunk, (c + 1) * chunk)
                z = _dot(wg, vb[rs, local]) + bs_ref[:, glob]
                s_ref[b, rs, local] = (u_ref[b, rs, local] * z).astype(BF16)

    for c in range(nb):
        acc = h_ref[:, cols[c]] + bo_ref[:, cols[c]]
        for b in range(nb):
            acc = acc + _dot(s_ref[b], wo_ref[b * tn:(b + 1) * tn, cols[c]])
        ho_ref[:, cols[c]] = acc


def _gmlp_prompt(h, wts, w_in_parts, w_out, g_layer, layer, seq_len):
    m, d = h.shape
    n_groups, chunk = wts["c_w_s"].shape[1], wts["c_w_s"].shape[2]
    tm, tn = min(GMLP_ROW_TILE, seq_len), min(COL_TILE, d)
    nb = d // tn
    tps = seq_len // tm
    assert seq_len % tm == 0 and tm % chunk == 0 and tn % (d // n_groups) == 0
    vec = lambda i: (layer, 0, 0)
    resident = pl.Buffered(1)
    return pl.pallas_call(
        functools.partial(_gmlp_prompt_kernel, chunk=chunk, n_groups=n_groups),
        grid=(m // tm,),
        in_specs=[
            pl.BlockSpec((tm, d), lambda i: (i, 0)),
            pl.BlockSpec((None, 1, d), lambda i: (g_layer, 0, 0)),
            pl.BlockSpec((None, d, d), lambda i: (w_in_parts[0][1], 0, 0), pipeline_mode=resident),
            pl.BlockSpec((None, d, d), lambda i: (w_in_parts[1][1], 0, 0), pipeline_mode=resident),
            pl.BlockSpec((None, 1, 2 * d), vec),
            pl.BlockSpec((None, 1, d), vec),
            pl.BlockSpec((None, 1, d), vec),
            pl.BlockSpec((None, n_groups, chunk, chunk), lambda i: (layer, 0, 0, 0)),
            pl.BlockSpec((None, chunk, d), vec),
            pl.BlockSpec((None, d, d), lambda i: (w_out[1], 0, 0), pipeline_mode=resident),
            pl.BlockSpec((None, 1, d), vec),
        ],
        out_specs=[
            pl.BlockSpec((tm, d), lambda i: (i, 0)),
            pl.BlockSpec((None, chunk, d), lambda i: (i // tps, 0, 0)),
        ],
        out_shape=[jax.ShapeDtypeStruct((m, d), F32), jax.ShapeDtypeStruct((m // seq_len, chunk, d), F32)],
        scratch_shapes=[
            pltpu.VMEM((nb, tm, tn), F32),
            pltpu.VMEM((nb, tm, tn), F32),
            pltpu.VMEM((nb, tm, tn), BF16),
        ],
        compiler_params=_params("arbitrary"),
        name="gmlp_prompt",
    )(h, wts["g_mix"], w_in_parts[0][0], w_in_parts[1][0], wts["c_b_in"], wts["c_ln_g"], wts["c_ln_b"],
      wts["c_w_s"], wts["c_bs_rows"], w_out[0], wts["c_b_out"])


def _dwconv_sample_kernel(z_ref, st_ref, w_ref, y_ref, ns_ref, *, n_steps):
    n_hist = st_ref.shape[0]
    rb = z_ref.shape[0] // n_steps
    slabs = [st_ref[k] for k in range(n_hist)] + [z_ref[t * rb:(t + 1) * rb, :] for t in range(n_steps)]
    for k in range(n_hist):
        ns_ref[k] = slabs[n_steps + k]
    y = _conv_slabs(slabs, w_ref, n_steps)
    for t in range(n_steps):
        y_ref[t * rb:(t + 1) * rb, :] = y[t]


def _dwconv_sample(z, state, w, layer, n_steps):
    m, d = z.shape
    width = w.shape[1]
    batch = state.shape[2]
    tc = min(CONV_COL_TILE, d)
    return pl.pallas_call(
        functools.partial(_dwconv_sample_kernel, n_steps=n_steps),
        grid=(d // tc,),
        in_specs=[
            pl.BlockSpec((m, tc), lambda c: (0, c)),
            pl.BlockSpec((None, width - 1, batch, tc), lambda c: (layer, 0, 0, c)),
            pl.BlockSpec((None, width, tc), lambda c: (layer, 0, c)),
        ],
        out_specs=[pl.BlockSpec((m, tc), lambda c: (0, c)), pl.BlockSpec((width - 1, batch, tc), lambda c: (0, 0, c))],
        out_shape=[jax.ShapeDtypeStruct((m, d), F32), jax.ShapeDtypeStruct((width - 1, batch, d), F32)],
        compiler_params=_params("parallel"),
        name="dwconv_sample",
    )(z, state, w)


def _out_proj_kernel(h_ref, w_ref, b_ref, *rest, kind, n_steps, emit):
    j = pl.program_id(1)
    s_ref = rest[-1]
    copy_ref = rest[-2] if emit else None
    rest = rest[:-2] if emit else rest[:-1]
    if kind == "conf":
        y_ref, bdw_ref, lg_ref, lb_ref, o_ref = rest
    else:
        u_ref, v_ref, lg_ref, lb_ref, coef_ref, bs_ref, o_ref, vln_ref = rest

    @pl.when(j == 0)
    def _():
        if kind == "conf":
            s_ref[...] = jax.nn.silu(_layernorm(y_ref[...] + bdw_ref[...], lg_ref[...], lb_ref[...])).astype(BF16)
        else:
            rb = v_ref.shape[0] // n_steps
            vln = _layernorm(v_ref[...], lg_ref[...], lb_ref[...])
            vln_ref[...] = vln
            vb = vln.astype(BF16).astype(F32)
            for t in range(n_steps):
                z = bs_ref[t:t + 1, :]
                for s in range(t + 1):
                    coef = coef_ref[t * n_steps + s:t * n_steps + s + 1, :].astype(BF16).astype(F32)
                    z = z + coef * vb[s * rb:(s + 1) * rb, :]
                s_ref[t * rb:(t + 1) * rb, :] = (u_ref[t * rb:(t + 1) * rb, :] * z).astype(BF16)

    o_ref[...] = h_ref[...] + _dot(s_ref[...], _weight_tile(w_ref, copy_ref)) + b_ref[...]


def _out_proj(h, w, b, layer, kind, tm, ins, in_specs_extra, extra_out=None, *, emit, n_steps=0):
    m, d = h.shape
    tn = min(COL_TILE, d)
    col = lambda i, j: j
    in_specs = [
        pl.BlockSpec((tm, tn), lambda i, j: (i, j)),
        _weight_spec(w, d, tn, col),
        pl.BlockSpec((None, 1, tn), lambda i, j: (layer, 0, j)),
    ] + in_specs_extra
    out_specs = [pl.BlockSpec((tm, tn), lambda i, j: (i, j))]
    out_shape = [jax.ShapeDtypeStruct((m, d), F32)]
    if extra_out is not None:
        out_specs.append(extra_out[0])
        out_shape.append(extra_out[1])
    if emit:
        assert m == tm
        out_specs.append(_bf16_copy_spec(d, tn, col))
        out_shape.append(jax.ShapeDtypeStruct((1, d, d), BF16))
    return pl.pallas_call(
        functools.partial(_out_proj_kernel, kind=kind, n_steps=n_steps, emit=emit),
        grid=(m // tm, d // tn), in_specs=in_specs, out_specs=out_specs, out_shape=out_shape,
        scratch_shapes=[pltpu.VMEM((tm, d), BF16)],
        compiler_params=_params("arbitrary", "arbitrary"),
        name=f"out_proj_{kind}",
    )(h, w[0], b, *ins)


def _ple_kernel(h_ref, g_ref, wg_ref, p_ref, wp_ref, *rest, final, tn):
    if final:
        gf_ref, o_ref, y_ref = rest
    else:
        (o_ref,) = rest
    d = h_ref.shape[1]
    xn = _rmsnorm(h_ref[...], g_ref[...]).astype(BF16)
    pb = p_ref[...].astype(BF16)
    for c0 in range(0, d, tn):
        gate = jax.nn.sigmoid(_dot(xn, wg_ref[:, c0:c0 + tn]))
        o_ref[:, c0:c0 + tn] = h_ref[:, c0:c0 + tn] + gate * _dot(pb, wp_ref[:, c0:c0 + tn])
    if final:
        y_ref[...] = _rmsnorm(o_ref[...], gf_ref[...])


def _ple(h, g, wg, p, wp, layer, g_final=None):
    m, d = h.shape
    pdim = p.shape[2]
    tm = min(ROW_TILE, m)
    final = g_final is not None
    whole = lambda i: 0
    in_specs = [
        pl.BlockSpec((tm, d), lambda i: (i, 0)),
        pl.BlockSpec((None, 1, d), lambda i: (layer, 0, 0)),
        _weight_spec(wg, d, d, whole),
        pl.BlockSpec((None, tm, pdim), lambda i: (layer, i, 0)),
        _weight_spec(wp, pdim, d, whole),
    ]
    args = [h, g, wg[0], p, wp[0]]
    out_specs = [pl.BlockSpec((tm, d), lambda i: (i, 0))]
    out_shape = [jax.ShapeDtypeStruct((m, d), F32)]
    if final:
        in_specs.append(pl.BlockSpec((1, d), lambda i: (0, 0)))
        args.append(g_final)
        out_specs.append(pl.BlockSpec((tm, d), lambda i: (i, 0)))
        out_shape.append(jax.ShapeDtypeStruct((m, d), F32))
    return pl.pallas_call(
        functools.partial(_ple_kernel, final=final, tn=min(COL_TILE, d)),
        grid=(m // tm,), in_specs=in_specs, out_specs=out_specs, out_shape=out_shape,
        compiler_params=_params("parallel"),
        name="ple_final" if final else "ple",
    )(*args)


def _ple_sample_kernel(h_ref, hcol_ref, g_ref, wg_ref, p_ref, wp_ref, o_ref, wg_copy_ref, wp_copy_ref, xn_ref):
    @pl.when(pl.program_id(0) == 0)
    def _():
        xn_ref[...] = _rmsnorm(h_ref[...], g_ref[...]).astype(BF16)

    gate = jax.nn.sigmoid(_dot(xn_ref[...], _weight_tile(wg_ref, wg_copy_ref)))
    o_ref[...] = hcol_ref[...] + gate * _dot(p_ref[...].astype(BF16), _weight_tile(wp_ref, wp_copy_ref))


def _ple_sample(h, g, wg, p, wp, layer):
    m, d = h.shape
    pdim = p.shape[2]
    tn = min(COL_TILE, d)
    col = lambda j: j
    return pl.pallas_call(
        _ple_sample_kernel,
        grid=(d // tn,),
        in_specs=[
            pl.BlockSpec((m, d), lambda j: (0, 0)),
            pl.BlockSpec((m, tn), lambda j: (0, j)),
            pl.BlockSpec((None, 1, d), lambda j: (layer, 0, 0)),
            _weight_spec(wg, d, tn, col),
            pl.BlockSpec((None, m, pdim), lambda j: (layer, 0, 0)),
            _weight_spec(wp, pdim, tn, col),
        ],
        out_specs=[pl.BlockSpec((m, tn), lambda j: (0, j)), _bf16_copy_spec(d, tn, col), _bf16_copy_spec(pdim, tn, col)],
        out_shape=[jax.ShapeDtypeStruct((m, d), F32), jax.ShapeDtypeStruct((1, d, d), BF16),
                   jax.ShapeDtypeStruct((1, pdim, d), BF16)],
        scratch_shapes=[pltpu.VMEM((m, d), BF16)],
        compiler_params=_params("arbitrary"),
        name="ple_sample",
    )(h, h, g, wg[0], p, wp[0])


def _final_norm_kernel(h_ref, g_ref, y_ref):
    y_ref[...] = _rmsnorm(h_ref[...], g_ref[...])


def _final_norm(h, g):
    m, d = h.shape
    tm = min(ROW_TILE, m)
    return pl.pallas_call(
        _final_norm_kernel,
        grid=(m // tm,),
        in_specs=[pl.BlockSpec((tm, d), lambda i: (i, 0)), pl.BlockSpec((1, d), lambda i: (0, 0))],
        out_specs=pl.BlockSpec((tm, d), lambda i: (i, 0)),
        out_shape=jax.ShapeDtypeStruct((m, d), F32),
        compiler_params=_params("parallel"),
        name="final_norm",
    )(h, g)


def _trunk(h, p, wts, weight, *, seq_len, n_steps, states):
    sample = states is not None
    depth = wts["g_mix"].shape[0]
    new_a, new_b, new_c, new_f, copies = [], [], [], [], {}
    for i in range(depth):
        kind, l = i % 3, i // 3
        m, d = h.shape
        vec = lambda i_, j_, l=l: (l, 0, 0)
        if kind == 0 and not sample:
            h, ns = _conformer_prompt(h, wts, [weight("a_w_in", l, 0, 2), weight("a_w_in", l, 1, 2)],
                                      weight("a_w_out", l), i, l, seq_len)
            new_a.append(ns)
        elif kind == 0:
            z, wa_copy, wg_copy = _in_proj(h, wts["g_mix"], i, [weight("a_w_in", l, 0, 2), weight("a_w_in", l, 1, 2)],
                                           wts["a_b_in"], l, "glu", True)
            conv, ns = _dwconv_sample(z, states[0], wts["a_w_dw"], l, n_steps)
            new_a.append(ns)
            h, wo_copy = _out_proj(
                h, weight("a_w_out", l), wts["a_b_out"], l, "conf", m,
                [conv, wts["a_b_dw"], wts["a_ln_g"], wts["a_ln_b"]],
                [pl.BlockSpec((m, d), lambda i_, j_: (0, 0))] + [pl.BlockSpec((None, 1, d), vec)] * 3, emit=True)
            copies["a_w_in", l], copies["a_w_out", l] = (wa_copy, wg_copy), (wo_copy,)
        elif kind == 1:
            res = _gated(h, wts["g_mix"], i, [weight("b_w_in", l, part, 3) for part in range(3)],
                         wts["b_w_dw"], wts["b_zero_bias"], weight("b_w_out", l), l, "short",
                         seq_len=seq_len, state=states[1] if sample else None, n_steps=n_steps)
            h, ns = res[:2]
            new_b.append(ns)
            if sample:
                copies["b_w_in", l], copies["b_w_out", l] = tuple(res[2:5]), (res[5],)
        elif not sample:
            h, nv = _gmlp_prompt(h, wts, [weight("c_w_in", l, 0, 2), weight("c_w_in", l, 1, 2)],
                                 weight("c_w_out", l), i, l, seq_len)
            new_c.append(nv)
        else:
            u, v, wu_copy, wv_copy = _in_proj(h, wts["g_mix"], i, [weight("c_w_in", l, 0, 2), weight("c_w_in", l, 1, 2)],
                                              wts["c_b_in"], l, "gelu2", True)
            full = pl.BlockSpec((m, d), lambda i_, j_: (0, 0))
            h, nv, wo_copy = _out_proj(
                h, weight("c_w_out", l), wts["c_b_out"], l, "gmlp_sample", m,
                [u, v, wts["c_ln_g"], wts["c_ln_b"], wts["c_coef"], wts["c_bs_steps"]],
                [full, full] + [pl.BlockSpec((None, 1, d), vec)] * 2
                + [pl.BlockSpec((None, n_steps * n_steps, d), vec), pl.BlockSpec((None, n_steps, d), vec)],
                (full, jax.ShapeDtypeStruct((m, d), F32)), emit=True, n_steps=n_steps)
            copies["c_w_in", l], copies["c_w_out", l] = (wu_copy, wv_copy), (wo_copy,)
            new_c.append(nv)
        res = _gated(h, wts["g_ffn"], i, [weight("f_w_gate", i), weight("f_w_up", i)],
                     wts["f_w_dw"], wts["f_b_dw"], weight("f_w_down", i), i, "ffn",
                     seq_len=seq_len, state=states[2] if sample else None, n_steps=n_steps)
        h, nf = res[:2]
        new_f.append(nf)
        if sample:
            copies["f_w_gate", i], copies["f_w_up", i], copies["f_w_down", i] = (res[2],), (res[3],), (res[4],)
            h, wg_copy, wp_copy = _ple_sample(h, wts["g_ple"], weight("ple_w_gate", i), p, weight("ple_w_proj", i), i)
            copies["ple_w_gate", i], copies["ple_w_proj", i] = (wg_copy,), (wp_copy,)
            if i == depth - 1:
                y = _final_norm(h, wts["g_final"])
        elif i == depth - 1:
            h, y = _ple(h, wts["g_ple"], weight("ple_w_gate", i), p, weight("ple_w_proj", i), i, wts["g_final"])
        else:
            (h,) = _ple(h, wts["g_ple"], weight("ple_w_gate", i), p, weight("ple_w_proj", i), i)
    return y, new_a, new_b, new_c, new_f, copies


def kernel(x_prompt, x_sample, p_prompt, p_sample, state_conformer, state_shortconv, state_ffn, g_mix, g_ffn, g_ple, g_final, a_w_in, a_b_in, a_w_dw, a_b_dw, a_ln_g, a_ln_b, a_w_out, a_b_out, b_w_in, b_w_dw, b_w_out, c_w_in, c_b_in, c_ln_g, c_ln_b, c_w_s, c_b_s, c_w_out, c_b_out, f_w_gate, f_w_dw, f_b_dw, f_w_up, f_w_down, ple_w_proj, ple_w_gate):
    bp, seq_len, d = x_prompt.shape
    bs, n_steps, _ = x_sample.shape
    depth = g_mix.shape[0]
    n_groups, chunk = c_w_s.shape[1], c_w_s.shape[2]
    gd = d // n_groups
    assert n_steps <= chunk

    row3 = lambda a: a[:, None, :]
    wts = dict(
        g_mix=row3(g_mix), g_ffn=row3(g_ffn), g_ple=row3(g_ple), g_final=g_final[None, :],
        a_b_in=row3(a_b_in), a_w_dw=a_w_dw, a_b_dw=row3(a_b_dw),
        a_ln_g=row3(a_ln_g), a_ln_b=row3(a_ln_b), a_b_out=row3(a_b_out),
        b_w_dw=b_w_dw, b_zero_bias=jnp.zeros((b_w_dw.shape[0], 1, d), F32),
        c_b_in=row3(c_b_in), c_ln_g=row3(c_ln_g), c_ln_b=row3(c_ln_b), c_w_s=c_w_s, c_b_out=row3(c_b_out),
        c_bs_rows=jnp.repeat(jnp.swapaxes(c_b_s, 1, 2), gd, axis=2),
        c_bs_steps=jnp.repeat(jnp.swapaxes(c_b_s, 1, 2)[:, :n_steps], gd, axis=2),
        c_coef=jnp.repeat(jnp.transpose(c_w_s[:, :, :n_steps, :n_steps], (0, 2, 3, 1)).reshape(
            c_w_s.shape[0], n_steps * n_steps, n_groups), gd, axis=2),
        f_w_dw=f_w_dw, f_b_dw=row3(f_b_dw),
    )
    big = dict(a_w_in=a_w_in, a_w_out=a_w_out, b_w_in=b_w_in, b_w_out=b_w_out, c_w_in=c_w_in, c_w_out=c_w_out,
               f_w_gate=f_w_gate, f_w_up=f_w_up, f_w_down=f_w_down, ple_w_gate=ple_w_gate, ple_w_proj=ple_w_proj)

    tmaj = lambda a: jnp.swapaxes(a, 1, 2)
    hs = jnp.swapaxes(x_sample, 0, 1).reshape(n_steps * bs, d)
    ps = tmaj(p_sample).reshape(depth, n_steps * bs, -1)
    states = (tmaj(state_conformer), tmaj(state_shortconv), tmaj(state_ffn))
    ys, a_s, b_s, c_s, f_s, copies = _trunk(
        hs, ps, wts, lambda name, layer, part=0, n_parts=1: (big[name], layer, part, n_parts),
        seq_len=None, n_steps=n_steps, states=states)

    hp = x_prompt.reshape(bp * seq_len, d)
    pp = p_prompt.reshape(depth, bp * seq_len, -1)
    yp, a_p, b_p, c_p, f_p, _ = _trunk(
        hp, pp, wts, lambda name, layer, part=0, n_parts=1: (copies[name, layer][part], 0, 0, 1),
        seq_len=seq_len, n_steps=None, states=None)

    tm = min(ROW_TILE, seq_len)
    tps = seq_len // tm

    def prompt_tail(tails, width):
        per_seq = tails.shape[0] // bp
        return tails[per_seq - 1::per_seq, SUBLANES - (width - 1):, :]

    def conformer_tail(tails):
        per_seq = tails.shape[0] // bp
        last = tails[per_seq - 1::per_seq]
        rows = jnp.swapaxes(last, 1, 2).reshape(bp, last.shape[2], d)
        return rows[:, rows.shape[1] - (wa - 1):]

    bmaj = lambda a: jnp.swapaxes(a, 0, 1)
    wa, wb, wf = a_w_dw.shape[1], b_w_dw.shape[1], f_w_dw.shape[1]
    return (
        yp.reshape(bp, seq_len, d),
        bmaj(ys.reshape(n_steps, bs, d)),
        jnp.stack([conformer_tail(t) for t in a_p]),
        jnp.stack([bmaj(s) for s in a_s]),
        jnp.stack([prompt_tail(t, wb) for t in b_p]),
        jnp.stack([bmaj(s) for s in b_s]),
        jnp.stack(c_p),
        jnp.stack([bmaj(v.reshape(n_steps, bs, d)) for v in c_s]),
        jnp.stack([prompt_tail(t, wf) for t in f_p]),
        jnp.stack([bmaj(s) for s in f_s]),
    )
```

```python
import functools

import jax
import jax.numpy as jnp
from jax import lax
from jax.experimental import pallas as pl
from jax.experimental.pallas import tpu as pltpu

EPS = 1e-6
F32 = jnp.float32
BF16 = jnp.bfloat16

ROW_TILE = 512
PROJ_ROW_TILE = 1024
COL_TILE = 512
CONFORMER_ROW_TILE = 256
GMLP_ROW_TILE = 256
SHORT_ROW_TILE = 256
CONV_COL_TILE = 256
SAMPLE_COL_TILE = 256
CONV_ROW_CHUNK = 64
GATE_ROW_CHUNK = 64
SUBLANES = 8
VMEM_LIMIT = 56 * 1024 * 1024


def _params(*sem):
    return pltpu.CompilerParams(dimension_semantics=sem, vmem_limit_bytes=VMEM_LIMIT)


def _rmsnorm(x, g):
    return x * lax.rsqrt(jnp.mean(x * x, axis=-1, keepdims=True) + EPS) * g


def _layernorm(x, g, b):
    xc = x - jnp.mean(x, axis=-1, keepdims=True)
    return xc * lax.rsqrt(jnp.mean(xc * xc, axis=-1, keepdims=True) + EPS) * g + b


def _dot(a, b):
    return jnp.dot(a, b, preferred_element_type=F32)


def _round_up(n, m):
    return -(-n // m) * m


def _weight_spec(operand, k, tn, col_of):
    arr, layer, part, n_parts = operand
    off = part * (arr.shape[2] // n_parts // tn)
    return pl.BlockSpec((None, k, tn), lambda *idx: (layer, 0, col_of(*idx) + off))


def _weight_row_spec(operand, tk, n):
    _, layer, _, n_parts = operand
    assert n_parts == 1
    return pl.BlockSpec((None, tk, n), lambda *idx: (layer, idx[-1], 0))


def _bf16_copy_spec(k, tn, col_of):
    return pl.BlockSpec((None, k, tn), lambda *idx: (0, 0, col_of(*idx)))


def _weight_tile(w_ref, copy_ref):
    if copy_ref is None:
        return w_ref[...]
    wb = w_ref[...].astype(BF16)
    copy_ref[...] = wb
    return wb


def _conv_slabs(slabs, w_ref, n_out):
    width = w_ref.shape[0]
    out = []
    for t in range(n_out):
        acc = None
        for k in range(width):
            term = w_ref[k:k + 1, :] * slabs[t + k]
            acc = term if acc is None else acc + term
        out.append(acc)
    return out


def _in_proj_kernel(h_ref, g_ref, w0_ref, w1_ref, b0_ref, b1_ref, *rest, kind, emit):
    outs, xn_ref = rest[:-1], rest[-1]

    @pl.when(pl.program_id(1) == 0)
    def _():
        xn_ref[...] = _rmsnorm(h_ref[...], g_ref[...]).astype(BF16)

    n_act = 1 if kind == "glu" else 2
    xn = xn_ref[...]
    w0 = _weight_tile(w0_ref, outs[n_act] if emit else None)
    w1 = _weight_tile(w1_ref, outs[n_act + 1] if emit else None)
    a = _dot(xn, w0) + b0_ref[...]
    b = _dot(xn, w1) + b1_ref[...]
    if kind == "glu":
        outs[0][...] = a * jax.nn.sigmoid(b)
    else:
        outs[0][...] = jax.nn.gelu(a)
        outs[1][...] = jax.nn.gelu(b)


def _in_proj(h, g, g_layer, w_parts, b, layer, kind, emit):
    m, d = h.shape
    tm, tn = min(PROJ_ROW_TILE, m), min(COL_TILE, d)
    nj = d // tn
    n_act = 1 if kind == "glu" else 2
    col = lambda i, j: j
    out_specs = [pl.BlockSpec((tm, tn), lambda i, j: (i, j))] * n_act
    out_shape = [jax.ShapeDtypeStruct((m, d), F32)] * n_act
    if emit:
        assert m == tm
        out_specs += [_bf16_copy_spec(d, tn, col)] * 2
        out_shape += [jax.ShapeDtypeStruct((1, d, d), BF16)] * 2
    return pl.pallas_call(
        functools.partial(_in_proj_kernel, kind=kind, emit=emit),
        grid=(m // tm, nj),
        in_specs=[
            pl.BlockSpec((tm, d), lambda i, j: (i, 0)),
            pl.BlockSpec((None, 1, d), lambda i, j: (g_layer, 0, 0)),
            _weight_spec(w_parts[0], d, tn, col),
            _weight_spec(w_parts[1], d, tn, col),
            pl.BlockSpec((None, 1, tn), lambda i, j: (layer, 0, j)),
            pl.BlockSpec((None, 1, tn), lambda i, j: (layer, 0, j + nj)),
        ],
        out_specs=out_specs, out_shape=out_shape,
        scratch_shapes=[pltpu.VMEM((tm, d), BF16)],
        compiler_params=_params("parallel", "arbitrary"),
        name=f"in_proj_{kind}",
    )(h, g, w_parts[0][0], w_parts[1][0], b, b)


def _gate(conv, bias, other, mode):
    if mode == "ffn":
        return jax.nn.silu(conv + bias) * other
    return other * conv


def _gated_prompt_kernel(*refs, n_in, mode, tiles_per_seq, row_chunk):
    h_ref, g_ref = refs[:2]
    w_in = refs[2:2 + n_in]
    wdw_ref, bdw_ref, w2_ref, ho_ref, tail_ref, xn_ref, ext_ref, oth_ref, hid_ref, carry_ref = refs[2 + n_in:]
    i, j = pl.program_id(0), pl.program_id(1)
    tm, tf = oth_ref.shape
    hist = SUBLANES
    width = wdw_ref.shape[0]
    base = hist - (width - 1)

    @pl.when(j == 0)
    def _():
        xn_ref[...] = _rmsnorm(h_ref[...], g_ref[...]).astype(BF16)
        ho_ref[...] = h_ref[...]

        @pl.when(i % tiles_per_seq == 0)
        def _():
            carry_ref[...] = jnp.zeros(carry_ref.shape, F32)

    xn = xn_ref[...]
    ext_ref[0:hist, :] = carry_ref[j]
    if mode == "ffn":
        ext_ref[hist:hist + tm, :] = _dot(xn, w_in[0][...])
        oth_ref[...] = _dot(xn, w_in[1][...])
    else:
        oth_ref[...] = _dot(xn, w_in[0][...])
        ext_ref[hist:hist + tm, :] = _dot(xn, w_in[1][...]) * _dot(xn, w_in[2][...])
    tail = ext_ref[tm:tm + hist, :]
    carry_ref[j] = tail
    tail_ref[...] = tail

    for r0 in range(0, tm, row_chunk):
        for c0 in range(0, tf, 128):
            cs = slice(c0, c0 + 128)
            conv = None
            for k in range(width):
                term = wdw_ref[k:k + 1, cs] * ext_ref[base + k + r0:base + k + r0 + row_chunk, cs]
                conv = term if conv is None else conv + term
            hid = _gate(conv, bdw_ref[:, cs], oth_ref[r0:r0 + row_chunk, cs], mode)
            hid_ref[r0:r0 + row_chunk, cs] = hid.astype(BF16)

    ho_ref[...] += _dot(hid_ref[...], w2_ref[...])


def _gated_sample_kernel(*refs, n_in, mode, n_steps):
    h_ref, g_ref = refs[:2]
    w_in = refs[2:2 + n_in]
    wdw_ref, bdw_ref, w2_ref, st_ref, ho_ref, ns_ref = refs[2 + n_in:8 + n_in]
    copies = refs[8 + n_in:9 + 2 * n_in]
    xn_ref = refs[-1]
    j = pl.program_id(0)
    n_hist = st_ref.shape[0]
    rb = h_ref.shape[0] // n_steps

    @pl.when(j == 0)
    def _():
        xn_ref[...] = _rmsnorm(h_ref[...], g_ref[...]).astype(BF16)
        ho_ref[...] = h_ref[...]

    xn = xn_ref[...]
    proj = [_dot(xn, _weight_tile(w, c)) for w, c in zip(w_in, copies[:n_in])]
    v, other = (proj[0], proj[1]) if mode == "ffn" else (proj[1] * proj[2], proj[0])
    slabs = [st_ref[k] for k in range(n_hist)] + [v[t * rb:(t + 1) * rb, :] for t in range(n_steps)]
    for k in range(n_hist):
        ns_ref[k] = slabs[n_steps + k]
    conv = jnp.concatenate(_conv_slabs(slabs, wdw_ref, n_steps), axis=0)
    hid = _gate(conv, bdw_ref[...], other, mode)
    ho_ref[...] += _dot(hid.astype(BF16), _weight_tile(w2_ref, copies[n_in]))


def _gated(h, g, g_layer, w_in_list, wdw, bdw, w2, layer, mode, *, seq_len=None, state=None, n_steps=None):
    m, d = h.shape
    hidden = wdw.shape[2]
    sample = state is not None
    tf = min(SAMPLE_COL_TILE if sample else COL_TILE, hidden)
    nj = hidden // tf
    width = wdw.shape[1]
    n_in = len(w_in_list)
    tm = m if sample else min(ROW_TILE, seq_len)
    if sample:
        row = lambda j: (0, 0)
        vec = lambda l: (lambda j: (l, 0, 0))
        hid_vec = lambda j: (layer, 0, j)
        grid = (nj,)
    else:
        row = lambda i, j: (i, 0)
        vec = lambda l: (lambda i, j: (l, 0, 0))
        hid_vec = lambda i, j: (layer, 0, j)
        grid = (m // tm, nj)
    col = lambda *idx: idx[-1]
    in_specs = [pl.BlockSpec((tm, d), row), pl.BlockSpec((None, 1, d), vec(g_layer))]
    in_specs += [_weight_spec(w, d, tf, col) for w in w_in_list]
    in_specs += [
        pl.BlockSpec((None, width, tf), hid_vec),
        pl.BlockSpec((None, 1, tf), hid_vec),
        _weight_row_spec(w2, tf, d),
    ]
    args = [h, g] + [w[0] for w in w_in_list] + [wdw, bdw, w2[0]]
    scratch = [pltpu.VMEM((tm, d), BF16)]
    if sample:
        batch = state.shape[2]
        in_specs.append(pl.BlockSpec((None, width - 1, batch, tf), lambda j: (layer, 0, 0, j)))
        args.append(state)
        out_specs = [pl.BlockSpec((tm, d), row), pl.BlockSpec((width - 1, batch, tf), lambda j: (0, 0, j))]
        out_shape = [jax.ShapeDtypeStruct((m, d), F32), jax.ShapeDtypeStruct((width - 1, batch, hidden), F32)]
        out_specs += [_bf16_copy_spec(d, tf, col)] * n_in + [pl.BlockSpec((None, tf, d), lambda j: (0, j, 0))]
        out_shape += [jax.ShapeDtypeStruct((1, d, hidden), BF16)] * n_in + [jax.ShapeDtypeStruct((1, hidden, d), BF16)]
        body = functools.partial(_gated_sample_kernel, n_in=n_in, mode=mode, n_steps=n_steps)
        sem = ("arbitrary",)
    else:
        assert seq_len % tm == 0 and width - 1 <= SUBLANES <= tm
        out_specs = [pl.BlockSpec((tm, d), row), pl.BlockSpec((None, SUBLANES, tf), lambda i, j: (i, 0, j))]
        out_shape = [jax.ShapeDtypeStruct((m, d), F32), jax.ShapeDtypeStruct((m // tm, SUBLANES, hidden), F32)]
        scratch += [pltpu.VMEM((SUBLANES + tm, tf), F32), pltpu.VMEM((tm, tf), F32), pltpu.VMEM((tm, tf), BF16),
                    pltpu.VMEM((nj, SUBLANES, tf), F32)]
        body = functools.partial(_gated_prompt_kernel, n_in=n_in, mode=mode, tiles_per_seq=seq_len // tm,
                                 row_chunk=min(GATE_ROW_CHUNK, tm))
        sem = ("arbitrary", "arbitrary")
    return pl.pallas_call(
        body, grid=grid, in_specs=in_specs, out_specs=out_specs, out_shape=out_shape,
        scratch_shapes=scratch, compiler_params=_params(*sem),
        name=f"gated_{mode}_{'sample' if sample else 'prompt'}",
    )(*args)


def _short_prompt_kernel(h_ref, g_ref, wb_ref, wc_ref, wx_ref, wdw_ref, wo_ref, ho_ref, tail_ref, ext_ref, oth_ref, hid_ref,
                         *, tiles_per_seq, row_chunk):
    i = pl.program_id(0)
    nb, tm, tn = oth_ref.shape
    hist = ext_ref.shape[1] - tm
    width = wdw_ref.shape[0]
    base = hist - (width - 1)
    cols = [slice(b * tn, (b + 1) * tn) for b in range(nb)]
    seq_start = i % tiles_per_seq == 0

    @pl.when(seq_start)
    def _():
        for b in range(nb):
            ext_ref[b, 0:hist, :] = jnp.zeros((hist, tn), F32)

    @pl.when(jnp.logical_not(seq_start))
    def _():
        for b in range(nb):
            ext_ref[b, 0:hist, :] = ext_ref[b, tm:tm + hist, :]

    xn = _rmsnorm(h_ref[...], g_ref[...]).astype(BF16)
    for b in range(nb):
        oth_ref[b] = _dot(xn, wb_ref[:, cols[b]])
        ext_ref[b, hist:hist + tm, :] = _dot(xn, wc_ref[:, cols[b]]) * _dot(xn, wx_ref[:, cols[b]])
        tail_ref[:, cols[b]] = ext_ref[b, tm:tm + hist, :]
        for r0 in range(0, tm, row_chunk):
            for c0 in range(0, tn, 128):
                cs = slice(c0, c0 + 128)
                conv = None
                for k in range(width):
                    term = (wdw_ref[k:k + 1, b * tn + c0:b * tn + c0 + 128]
                            * ext_ref[b, base + k + r0:base + k + r0 + row_chunk, cs])
                    conv = term if conv is None else conv + term
                hid_ref[b, r0:r0 + row_chunk, cs] = (oth_ref[b, r0:r0 + row_chunk, cs] * conv).astype(BF16)

    for c in range(nb):
        acc = h_ref[:, cols[c]]
        for b in range(nb):
            acc = acc + _dot(hid_ref[b], wo_ref[b * tn:(b + 1) * tn, cols[c]])
        ho_ref[:, cols[c]] = acc


def _short_prompt(h, wts, w_in_parts, w_out, g_layer, layer, seq_len):
    m, d = h.shape
    width = wts["b_w_dw"].shape[1]
    tm, tn = min(SHORT_ROW_TILE, seq_len), min(COL_TILE, d)
    nb = d // tn
    hist = SUBLANES
    assert seq_len % tm == 0 and width - 1 <= hist <= tm
    resident = pl.Buffered(1)
    whole = lambda operand: pl.BlockSpec((None, d, d), lambda i: (operand[1], 0, 0), pipeline_mode=resident)
    return pl.pallas_call(
        functools.partial(_short_prompt_kernel, tiles_per_seq=seq_len // tm, row_chunk=min(GATE_ROW_CHUNK, tm)),
        grid=(m // tm,),
        in_specs=[
            pl.BlockSpec((tm, d), lambda i: (i, 0)),
            pl.BlockSpec((None, 1, d), lambda i: (g_layer, 0, 0)),
            whole(w_in_parts[0]), whole(w_in_parts[1]), whole(w_in_parts[2]),
            pl.BlockSpec((None, width, d), lambda i: (layer, 0, 0)),
            whole(w_out),
        ],
        out_specs=[pl.BlockSpec((tm, d), lambda i: (i, 0)), pl.BlockSpec((None, hist, d), lambda i: (i, 0, 0))],
        out_shape=[jax.ShapeDtypeStruct((m, d), F32), jax.ShapeDtypeStruct((m // tm, hist, d), F32)],
        scratch_shapes=[
            pltpu.VMEM((nb, hist + tm, tn), F32),
            pltpu.VMEM((nb, tm, tn), F32),
            pltpu.VMEM((nb, tm, tn), BF16),
        ],
        compiler_params=_params("arbitrary"),
        name="short_prompt",
    )(h, wts["g_mix"], w_in_parts[0][0], w_in_parts[1][0], w_in_parts[2][0], wts["b_w_dw"], w_out[0])


def _conformer_prompt_kernel(h_ref, hprev_ref, g_ref, wa_ref, wg_ref, bin_ref, wdw_ref, bdw_ref, lng_ref, lnb_ref, wo_ref, bo_ref,
                             ho_ref, ztail_ref, zext_ref, sh_ref, y_ref, s_ref, *, n_tiles, tiles_per_seq, row_chunk):
    i = pl.program_id(0)
    nb, tm, tn = y_ref.shape
    hist = zext_ref.shape[1] - tm
    width = wdw_ref.shape[0]
    base = hist - (width - 1)
    d = nb * tn
    cols = [slice(b * tn, (b + 1) * tn) for b in range(nb)]
    seq_start = jnp.minimum(i, n_tiles - 1) % tiles_per_seq == 0

    @pl.when(i == 0)
    def _():
        y_ref[...] = jnp.zeros(y_ref.shape, F32)

    @pl.when(seq_start)
    def _():
        for b in range(nb):
            zext_ref[b, 0:hist, :] = jnp.zeros((hist, tn), F32)

    @pl.when(jnp.logical_not(seq_start))
    def _():
        for b in range(nb):
            zext_ref[b, 0:hist, :] = zext_ref[b, tm:tm + hist, :]

    def in_proj_block(xn, b):
        a = _dot(xn, wa_ref[:, cols[b]]) + bin_ref[:, cols[b]]
        g = _dot(xn, wg_ref[:, cols[b]]) + bin_ref[:, d + b * tn:d + (b + 1) * tn]
        zext_ref[b, hist:hist + tm, :] = a * jax.nn.sigmoid(g)

    def conv_block(b):
        n_shifted = hist + tm - SUBLANES
        for s in range(1, SUBLANES):
            sh_ref[s - 1, 0:n_shifted, :] = zext_ref[b, s:s + n_shifted, :]
        for c0 in range(0, tn, 128):
            cs = slice(c0, c0 + 128)
            w_rows = [wdw_ref[k:k + 1, b * tn + c0:b * tn + c0 + 128] for k in range(width)]
            for r0 in range(0, tm, row_chunk):
                acc = None
                for k in range(width):
                    q, s = divmod(base + k, SUBLANES)
                    rows = slice(r0 + q * SUBLANES, r0 + q * SUBLANES + row_chunk)
                    src = zext_ref[b, rows, cs] if s == 0 else sh_ref[s - 1, rows, cs]
                    term = w_rows[k] * src
                    acc = term if acc is None else acc + term
                y_ref[b, r0:r0 + row_chunk, cs] = acc

    yb = [y_ref[b] + bdw_ref[:, cols[b]] for b in range(nb)]
    mean = sum(jnp.sum(v, axis=-1, keepdims=True) for v in yb) / d
    yc = [v - mean for v in yb]
    var = sum(jnp.sum(v * v, axis=-1, keepdims=True) for v in yc) / d
    rstd = lax.rsqrt(var + EPS)
    for b in range(nb):
        s_ref[b] = jax.nn.silu(yc[b] * rstd * lng_ref[:, cols[b]] + lnb_ref[:, cols[b]]).astype(BF16)

    xn = _rmsnorm(h_ref[...], g_ref[...]).astype(BF16)
    in_proj_block(xn, 0)
    for b in range(1, nb):
        conv_block(b - 1)
        in_proj_block(xn, b)
    conv_block(nb - 1)
    for b in range(nb):
        ztail_ref[b] = zext_ref[b, tm:tm + hist, :]

    for c in range(nb):
        acc = hprev_ref[:, cols[c]] + bo_ref[:, cols[c]]
        for b in range(nb):
            acc = acc + _dot(s_ref[b], wo_ref[b * tn:(b + 1) * tn, cols[c]])
        ho_ref[:, cols[c]] = acc


def _conformer_prompt(h, wts, w_in_parts, w_out, g_layer, layer, seq_len):
    m, d = h.shape
    width = wts["a_w_dw"].shape[1]
    tm, tn = min(CONFORMER_ROW_TILE, seq_len), min(COL_TILE, d)
    nb = d // tn
    n_tiles = m // tm
    hist = _round_up(width - 1, SUBLANES)
    assert seq_len % tm == 0 and hist <= tm
    vec = lambda i: (layer, 0, 0)
    tile = lambda i: (jnp.minimum(i, n_tiles - 1), 0)
    prev = lambda i: (jnp.maximum(i - 1, 0), 0)
    resident = pl.Buffered(1)
    return pl.pallas_call(
        functools.partial(_conformer_prompt_kernel, n_tiles=n_tiles, tiles_per_seq=seq_len // tm,
                          row_chunk=min(CONV_ROW_CHUNK, tm)),
        grid=(n_tiles + 1,),
        in_specs=[
            pl.BlockSpec((tm, d), tile),
            pl.BlockSpec((tm, d), prev),
            pl.BlockSpec((None, 1, d), lambda i: (g_layer, 0, 0)),
            pl.BlockSpec((None, d, d), lambda i: (w_in_parts[0][1], 0, 0), pipeline_mode=resident),
            pl.BlockSpec((None, d, d), lambda i: (w_in_parts[1][1], 0, 0), pipeline_mode=resident),
            pl.BlockSpec((None, 1, 2 * d), vec),
            pl.BlockSpec((None, width, d), vec),
            pl.BlockSpec((None, 1, d), vec),
            pl.BlockSpec((None, 1, d), vec),
            pl.BlockSpec((None, 1, d), vec),
            pl.BlockSpec((None, d, d), lambda i: (w_out[1], 0, 0), pipeline_mode=resident),
            pl.BlockSpec((None, 1, d), vec),
        ],
        out_specs=[
            pl.BlockSpec((tm, d), prev),
            pl.BlockSpec((None, nb, hist, tn), lambda i: (jnp.minimum(i, n_tiles - 1), 0, 0, 0)),
        ],
        out_shape=[jax.ShapeDtypeStruct((m, d), F32), jax.ShapeDtypeStruct((n_tiles, nb, hist, tn), F32)],
        scratch_shapes=[
            pltpu.VMEM((nb, hist + tm, tn), F32),
            pltpu.VMEM((SUBLANES - 1, hist + tm, tn), F32),
            pltpu.VMEM((nb, tm, tn), F32),
            pltpu.VMEM((nb, tm, tn), BF16),
        ],
        compiler_params=_params("arbitrary"),
        name="conformer_prompt",
    )(h, h, wts["g_mix"], w_in_parts[0][0], w_in_parts[1][0], wts["a_b_in"], wts["a_w_dw"], wts["a_b_dw"], wts["a_ln_g"],
      wts["a_ln_b"], w_out[0], wts["a_b_out"])


def _gmlp_prompt_kernel(h_ref, g_ref, wu_ref, wv_ref, bin_ref, lng_ref, lnb_ref, ws_ref, bs_ref, wo_ref, bo_ref,
                        ho_ref, vlast_ref, u_ref, v_ref, s_ref, *, chunk, n_groups):
    nb, tm, tn = u_ref.shape
    d = nb * tn
    gd = d // n_groups
    groups_per_block = tn // gd
    cols = [slice(b * tn, (b + 1) * tn) for b in range(nb)]

    xn = _rmsnorm(h_ref[...], g_ref[...]).astype(BF16)
    for b in range(nb):
        u_ref[b] = jax.nn.gelu(_dot(xn, wu_ref[:, cols[b]]) + bin_ref[:, cols[b]])
        v_ref[b] = jax.nn.gelu(_dot(xn, wv_ref[:, cols[b]]) + bin_ref[:, d + b * tn:d + (b + 1) * tn])

    mean = sum(jnp.sum(v_ref[b], axis=-1, keepdims=True) for b in range(nb)) / d
    var = sum(jnp.sum((v_ref[b] - mean) ** 2, axis=-1, keepdims=True) for b in range(nb)) / d
    rstd = lax.rsqrt(var + EPS)
    rows = lax.broadcasted_iota(jnp.int32, (chunk, chunk), 0)
    lanes = lax.broadcasted_iota(jnp.int32, (chunk, chunk), 1)
    tril = (rows >= lanes).astype(F32)
    for b in range(nb):
        vln = (v_ref[b] - mean) * rstd * lng_ref[:, cols[b]] + lnb_ref[:, cols[b]]
        vlast_ref[:, cols[b]] = vln[tm - chunk:, :]
        vb = vln.astype(BF16)
        for gl in range(groups_per_block):
            g = b * groups_per_block + gl
            wg = (ws_ref[g] * tril).astype(BF16)
            local, glob = slice(gl * gd, (gl + 1) * gd), slice(g * gd, (g + 1) * gd)
            for c in range(tm // chunk):
                rs = slice(c * chunk, (c + 1) * chunk)
                z = _dot(wg, vb[rs, local]) + bs_ref[:, glob]
                s_ref[b, rs, local] = (u_ref[b, rs, local] * z).astype(BF16)

    for c in range(nb):
        acc = h_ref[:, cols[c]] + bo_ref[:, cols[c]]
        for b in range(nb):
            acc = acc + _dot(s_ref[b], wo_ref[b * tn:(b + 1) * tn, cols[c]])
        ho_ref[:, cols[c]] = acc


def _gmlp_prompt(h, wts, w_in_parts, w_out, g_layer, layer, seq_len):
    m, d = h.shape
    n_groups, chunk = wts["c_w_s"].shape[1], wts["c_w_s"].shape[2]
    tm, tn = min(GMLP_ROW_TILE, seq_len), min(COL_TILE, d)
    nb = d // tn
    tps = seq_len // tm
    assert seq_len % tm == 0 and tm % chunk == 0 and tn % (d // n_groups) == 0
    vec = lambda i: (layer, 0, 0)
    resident = pl.Buffered(1)
    return pl.pallas_call(
        functools.partial(_gmlp_prompt_kernel, chunk=chunk, n_groups=n_groups),
        grid=(m // tm,),
        in_specs=[
            pl.BlockSpec((tm, d), lambda i: (i, 0)),
            pl.BlockSpec((None, 1, d), lambda i: (g_layer, 0, 0)),
            pl.BlockSpec((None, d, d), lambda i: (w_in_parts[0][1], 0, 0), pipeline_mode=resident),
            pl.BlockSpec((None, d, d), lambda i: (w_in_parts[1][1], 0, 0), pipeline_mode=resident),
            pl.BlockSpec((None, 1, 2 * d), vec),
            pl.BlockSpec((None, 1, d), vec),
            pl.BlockSpec((None, 1, d), vec),
            pl.BlockSpec((None, n_groups, chunk, chunk), lambda i: (layer, 0, 0, 0)),
            pl.BlockSpec((None, chunk, d), vec),
            pl.BlockSpec((None, d, d), lambda i: (w_out[1], 0, 0), pipeline_mode=resident),
            pl.BlockSpec((None, 1, d), vec),
        ],
        out_specs=[
            pl.BlockSpec((tm, d), lambda i: (i, 0)),
            pl.BlockSpec((None, chunk, d), lambda i: (i // tps, 0, 0)),
        ],
        out_shape=[jax.ShapeDtypeStruct((m, d), F32), jax.ShapeDtypeStruct((m // seq_len, chunk, d), F32)],
        scratch_shapes=[
            pltpu.VMEM((nb, tm, tn), F32),
            pltpu.VMEM((nb, tm, tn), F32),
            pltpu.VMEM((nb, tm, tn), BF16),
        ],
        compiler_params=_params("arbitrary"),
        name="gmlp_prompt",
    )(h, wts["g_mix"], w_in_parts[0][0], w_in_parts[1][0], wts["c_b_in"], wts["c_ln_g"], wts["c_ln_b"],
      wts["c_w_s"], wts["c_bs_rows"], w_out[0], wts["c_b_out"])


def _dwconv_sample_kernel(z_ref, st_ref, w_ref, y_ref, ns_ref, *, n_steps):
    n_hist = st_ref.shape[0]
    rb = z_ref.shape[0] // n_steps
    slabs = [st_ref[k] for k in range(n_hist)] + [z_ref[t * rb:(t + 1) * rb, :] for t in range(n_steps)]
    for k in range(n_hist):
        ns_ref[k] = slabs[n_steps + k]
    y = _conv_slabs(slabs, w_ref, n_steps)
    for t in range(n_steps):
        y_ref[t * rb:(t + 1) * rb, :] = y[t]


def _dwconv_sample(z, state, w, layer, n_steps):
    m, d = z.shape
    width = w.shape[1]
    batch = state.shape[2]
    tc = min(CONV_COL_TILE, d)
    return pl.pallas_call(
        functools.partial(_dwconv_sample_kernel, n_steps=n_steps),
        grid=(d // tc,),
        in_specs=[
            pl.BlockSpec((m, tc), lambda c: (0, c)),
            pl.BlockSpec((None, width - 1, batch, tc), lambda c: (layer, 0, 0, c)),
            pl.BlockSpec((None, width, tc), lambda c: (layer, 0, c)),
        ],
        out_specs=[pl.BlockSpec((m, tc), lambda c: (0, c)), pl.BlockSpec((width - 1, batch, tc), lambda c: (0, 0, c))],
        out_shape=[jax.ShapeDtypeStruct((m, d), F32), jax.ShapeDtypeStruct((width - 1, batch, d), F32)],
        compiler_params=_params("parallel"),
        name="dwconv_sample",
    )(z, state, w)


def _out_proj_kernel(h_ref, w_ref, b_ref, *rest, kind, n_steps, emit):
    j = pl.program_id(1)
    s_ref = rest[-1]
    copy_ref = rest[-2] if emit else None
    rest = rest[:-2] if emit else rest[:-1]
    if kind == "conf":
        y_ref, bdw_ref, lg_ref, lb_ref, o_ref = rest
    else:
        u_ref, v_ref, lg_ref, lb_ref, coef_ref, bs_ref, o_ref, vln_ref = rest

    @pl.when(j == 0)
    def _():
        if kind == "conf":
            s_ref[...] = jax.nn.silu(_layernorm(y_ref[...] + bdw_ref[...], lg_ref[...], lb_ref[...])).astype(BF16)
        else:
            rb = v_ref.shape[0] // n_steps
            vln = _layernorm(v_ref[...], lg_ref[...], lb_ref[...])
            vln_ref[...] = vln
            vb = vln.astype(BF16).astype(F32)
            for t in range(n_steps):
                z = bs_ref[t:t + 1, :]
                for s in range(t + 1):
                    coef = coef_ref[t * n_steps + s:t * n_steps + s + 1, :].astype(BF16).astype(F32)
                    z = z + coef * vb[s * rb:(s + 1) * rb, :]
                s_ref[t * rb:(t + 1) * rb, :] = (u_ref[t * rb:(t + 1) * rb, :] * z).astype(BF16)

    o_ref[...] = h_ref[...] + _dot(s_ref[...], _weight_tile(w_ref, copy_ref)) + b_ref[...]


def _out_proj(h, w, b, layer, kind, tm, ins, in_specs_extra, extra_out=None, *, emit, n_steps=0):
    m, d = h.shape
    tn = min(COL_TILE, d)
    col = lambda i, j: j
    in_specs = [
        pl.BlockSpec((tm, tn), lambda i, j: (i, j)),
        _weight_spec(w, d, tn, col),
        pl.BlockSpec((None, 1, tn), lambda i, j: (layer, 0, j)),
    ] + in_specs_extra
    out_specs = [pl.BlockSpec((tm, tn), lambda i, j: (i, j))]
    out_shape = [jax.ShapeDtypeStruct((m, d), F32)]
    if extra_out is not None:
        out_specs.append(extra_out[0])
        out_shape.append(extra_out[1])
    if emit:
        assert m == tm
        out_specs.append(_bf16_copy_spec(d, tn, col))
        out_shape.append(jax.ShapeDtypeStruct((1, d, d), BF16))
    return pl.pallas_call(
        functools.partial(_out_proj_kernel, kind=kind, n_steps=n_steps, emit=emit),
        grid=(m // tm, d // tn), in_specs=in_specs, out_specs=out_specs, out_shape=out_shape,
        scratch_shapes=[pltpu.VMEM((tm, d), BF16)],
        compiler_params=_params("arbitrary", "arbitrary"),
        name=f"out_proj_{kind}",
    )(h, w[0], b, *ins)


def _ple_kernel(h_ref, g_ref, wg_ref, p_ref, wp_ref, *rest, final, tn):
    if final:
        gf_ref, o_ref, y_ref = rest
    else:
        (o_ref,) = rest
    d = h_ref.shape[1]
    xn = _rmsnorm(h_ref[...], g_ref[...]).astype(BF16)
    pb = p_ref[...].astype(BF16)
    for c0 in range(0, d, tn):
        gate = jax.nn.sigmoid(_dot(xn, wg_ref[:, c0:c0 + tn]))
        o_ref[:, c0:c0 + tn] = h_ref[:, c0:c0 + tn] + gate * _dot(pb, wp_ref[:, c0:c0 + tn])
    if final:
        y_ref[...] = _rmsnorm(o_ref[...], gf_ref[...])


def _ple(h, g, wg, p, wp, layer, g_final=None):
    m, d = h.shape
    pdim = p.shape[2]
    tm = min(ROW_TILE, m)
    final = g_final is not None
    whole = lambda i: 0
    in_specs = [
        pl.BlockSpec((tm, d), lambda i: (i, 0)),
        pl.BlockSpec((None, 1, d), lambda i: (layer, 0, 0)),
        _weight_spec(wg, d, d, whole),
        pl.BlockSpec((None, tm, pdim), lambda i: (layer, i, 0)),
        _weight_spec(wp, pdim, d, whole),
    ]
    args = [h, g, wg[0], p, wp[0]]
    out_specs = [pl.BlockSpec((tm, d), lambda i: (i, 0))]
    out_shape = [jax.ShapeDtypeStruct((m, d), F32)]
    if final:
        in_specs.append(pl.BlockSpec((1, d), lambda i: (0, 0)))
        args.append(g_final)
        out_specs.append(pl.BlockSpec((tm, d), lambda i: (i, 0)))
        out_shape.append(jax.ShapeDtypeStruct((m, d), F32))
    return pl.pallas_call(
        functools.partial(_ple_kernel, final=final, tn=min(COL_TILE, d)),
        grid=(m // tm,), in_specs=in_specs, out_specs=out_specs, out_shape=out_shape,
        compiler_params=_params("parallel"),
        name="ple_final" if final else "ple",
    )(*args)


def _ple_sample_kernel(h_ref, hcol_ref, g_ref, wg_ref, p_ref, wp_ref, o_ref, wg_copy_ref, wp_copy_ref, xn_ref):
    @pl.when(pl.program_id(0) == 0)
    def _():
        xn_ref[...] = _rmsnorm(h_ref[...], g_ref[...]).astype(BF16)

    gate = jax.nn.sigmoid(_dot(xn_ref[...], _weight_tile(wg_ref, wg_copy_ref)))
    o_ref[...] = hcol_ref[...] + gate * _dot(p_ref[...].astype(BF16), _weight_tile(wp_ref, wp_copy_ref))


def _ple_sample(h, g, wg, p, wp, layer):
    m, d = h.shape
    pdim = p.shape[2]
    tn = min(COL_TILE, d)
    col = lambda j: j
    return pl.pallas_call(
        _ple_sample_kernel,
        grid=(d // tn,),
        in_specs=[
            pl.BlockSpec((m, d), lambda j: (0, 0)),
            pl.BlockSpec((m, tn), lambda j: (0, j)),
            pl.BlockSpec((None, 1, d), lambda j: (layer, 0, 0)),
            _weight_spec(wg, d, tn, col),
            pl.BlockSpec((None, m, pdim), lambda j: (layer, 0, 0)),
            _weight_spec(wp, pdim, tn, col),
        ],
        out_specs=[pl.BlockSpec((m, tn), lambda j: (0, j)), _bf16_copy_spec(d, tn, col), _bf16_copy_spec(pdim, tn, col)],
        out_shape=[jax.ShapeDtypeStruct((m, d), F32), jax.ShapeDtypeStruct((1, d, d), BF16),
                   jax.ShapeDtypeStruct((1, pdim, d), BF16)],
        scratch_shapes=[pltpu.VMEM((m, d), BF16)],
        compiler_params=_params("arbitrary"),
        name="ple_sample",
    )(h, h, g, wg[0], p, wp[0])


def _final_norm_kernel(h_ref, g_ref, y_ref):
    y_ref[...] = _rmsnorm(h_ref[...], g_ref[...])


def _final_norm(h, g):
    m, d = h.shape
    tm = min(ROW_TILE, m)
    return pl.pallas_call(
        _final_norm_kernel,
        grid=(m // tm,),
        in_specs=[pl.BlockSpec((tm, d), lambda i: (i, 0)), pl.BlockSpec((1, d), lambda i: (0, 0))],
        out_specs=pl.BlockSpec((tm, d), lambda i: (i, 0)),
        out_shape=jax.ShapeDtypeStruct((m, d), F32),
        compiler_params=_params("parallel"),
        name="final_norm",
    )(h, g)


def _trunk(h, p, wts, weight, *, seq_len, n_steps, states):
    sample = states is not None
    depth = wts["g_mix"].shape[0]
    new_a, new_b, new_c, new_f, copies = [], [], [], [], {}
    for i in range(depth):
        kind, l = i % 3, i // 3
        m, d = h.shape
        vec = lambda i_, j_, l=l: (l, 0, 0)
        if kind == 0 and not sample:
            h, ns = _conformer_prompt(h, wts, [weight("a_w_in", l, 0, 2), weight("a_w_in", l, 1, 2)],
                                      weight("a_w_out", l), i, l, seq_len)
            new_a.append(ns)
        elif kind == 0:
            z, wa_copy, wg_copy = _in_proj(h, wts["g_mix"], i, [weight("a_w_in", l, 0, 2), weight("a_w_in", l, 1, 2)],
                                           wts["a_b_in"], l, "glu", True)
            conv, ns = _dwconv_sample(z, states[0], wts["a_w_dw"], l, n_steps)
            new_a.append(ns)
            h, wo_copy = _out_proj(
                h, weight("a_w_out", l), wts["a_b_out"], l, "conf", m,
                [conv, wts["a_b_dw"], wts["a_ln_g"], wts["a_ln_b"]],
                [pl.BlockSpec((m, d), lambda i_, j_: (0, 0))] + [pl.BlockSpec((None, 1, d), vec)] * 3, emit=True)
            copies["a_w_in", l], copies["a_w_out", l] = (wa_copy, wg_copy), (wo_copy,)
        elif kind == 1 and not sample:
            h, ns = _short_prompt(h, wts, [weight("b_w_in", l, part, 3) for part in range(3)],
                                  weight("b_w_out", l), i, l, seq_len)
            new_b.append(ns)
        elif kind == 1:
            res = _gated(h, wts["g_mix"], i, [weight("b_w_in", l, part, 3) for part in range(3)],
                         wts["b_w_dw"], wts["b_zero_bias"], weight("b_w_out", l), l, "short",
                         state=states[1], n_steps=n_steps)
            h, ns = res[:2]
            new_b.append(ns)
            copies["b_w_in", l], copies["b_w_out", l] = tuple(res[2:5]), (res[5],)
        elif not sample:
            h, nv = _gmlp_prompt(h, wts, [weight("c_w_in", l, 0, 2), weight("c_w_in", l, 1, 2)],
                                 weight("c_w_out", l), i, l, seq_len)
            new_c.append(nv)
        else:
            u, v, wu_copy, wv_copy = _in_proj(h, wts["g_mix"], i, [weight("c_w_in", l, 0, 2), weight("c_w_in", l, 1, 2)],
                                              wts["c_b_in"], l, "gelu2", True)
            full = pl.BlockSpec((m, d), lambda i_, j_: (0, 0))
            h, nv, wo_copy = _out_proj(
                h, weight("c_w_out", l), wts["c_b_out"], l, "gmlp_sample", m,
                [u, v, wts["c_ln_g"], wts["c_ln_b"], wts["c_coef"], wts["c_bs_steps"]],
                [full, full] + [pl.BlockSpec((None, 1, d), vec)] * 2
                + [pl.BlockSpec((None, n_steps * n_steps, d), vec), pl.BlockSpec((None, n_steps, d), vec)],
                (full, jax.ShapeDtypeStruct((m, d), F32)), emit=True, n_steps=n_steps)
            copies["c_w_in", l], copies["c_w_out", l] = (wu_copy, wv_copy), (wo_copy,)
            new_c.append(nv)
        res = _gated(h, wts["g_ffn"], i, [weight("f_w_gate", i), weight("f_w_up", i)],
                     wts["f_w_dw"], wts["f_b_dw"], weight("f_w_down", i), i, "ffn",
                     seq_len=seq_len, state=states[2] if sample else None, n_steps=n_steps)
        h, nf = res[:2]
        new_f.append(nf)
        if sample:
            copies["f_w_gate", i], copies["f_w_up", i], copies["f_w_down", i] = (res[2],), (res[3],), (res[4],)
            h, wg_copy, wp_copy = _ple_sample(h, wts["g_ple"], weight("ple_w_gate", i), p, weight("ple_w_proj", i), i)
            copies["ple_w_gate", i], copies["ple_w_proj", i] = (wg_copy,), (wp_copy,)
            if i == depth - 1:
                y = _final_norm(h, wts["g_final"])
        elif i == depth - 1:
            h, y = _ple(h, wts["g_ple"], weight("ple_w_gate", i), p, weight("ple_w_proj", i), i, wts["g_final"])
        else:
            (h,) = _ple(h, wts["g_ple"], weight("ple_w_gate", i), p, weight("ple_w_proj", i), i)
    return y, new_a, new_b, new_c, new_f, copies


def kernel(x_prompt, x_sample, p_prompt, p_sample, state_conformer, state_shortconv, state_ffn, g_mix, g_ffn, g_ple, g_final, a_w_in, a_b_in, a_w_dw, a_b_dw, a_ln_g, a_ln_b, a_w_out, a_b_out, b_w_in, b_w_dw, b_w_out, c_w_in, c_b_in, c_ln_g, c_ln_b, c_w_s, c_b_s, c_w_out, c_b_out, f_w_gate, f_w_dw, f_b_dw, f_w_up, f_w_down, ple_w_proj, ple_w_gate):
    bp, seq_len, d = x_prompt.shape
    bs, n_steps, _ = x_sample.shape
    depth = g_mix.shape[0]
    n_groups, chunk = c_w_s.shape[1], c_w_s.shape[2]
    gd = d // n_groups
    assert n_steps <= chunk

    row3 = lambda a: a[:, None, :]
    wts = dict(
        g_mix=row3(g_mix), g_ffn=row3(g_ffn), g_ple=row3(g_ple), g_final=g_final[None, :],
        a_b_in=row3(a_b_in), a_w_dw=a_w_dw, a_b_dw=row3(a_b_dw),
        a_ln_g=row3(a_ln_g), a_ln_b=row3(a_ln_b), a_b_out=row3(a_b_out),
        b_w_dw=b_w_dw, b_zero_bias=jnp.zeros((b_w_dw.shape[0], 1, d), F32),
        c_b_in=row3(c_b_in), c_ln_g=row3(c_ln_g), c_ln_b=row3(c_ln_b), c_w_s=c_w_s, c_b_out=row3(c_b_out),
        c_bs_rows=jnp.repeat(jnp.swapaxes(c_b_s, 1, 2), gd, axis=2),
        c_bs_steps=jnp.repeat(jnp.swapaxes(c_b_s, 1, 2)[:, :n_steps], gd, axis=2),
        c_coef=jnp.repeat(jnp.transpose(c_w_s[:, :, :n_steps, :n_steps], (0, 2, 3, 1)).reshape(
            c_w_s.shape[0], n_steps * n_steps, n_groups), gd, axis=2),
        f_w_dw=f_w_dw, f_b_dw=row3(f_b_dw),
    )
    big = dict(a_w_in=a_w_in, a_w_out=a_w_out, b_w_in=b_w_in, b_w_out=b_w_out, c_w_in=c_w_in, c_w_out=c_w_out,
               f_w_gate=f_w_gate, f_w_up=f_w_up, f_w_down=f_w_down, ple_w_gate=ple_w_gate, ple_w_proj=ple_w_proj)

    tmaj = lambda a: jnp.swapaxes(a, 1, 2)
    hs = jnp.swapaxes(x_sample, 0, 1).reshape(n_steps * bs, d)
    ps = tmaj(p_sample).reshape(depth, n_steps * bs, -1)
    states = (tmaj(state_conformer), tmaj(state_shortconv), tmaj(state_ffn))
    ys, a_s, b_s, c_s, f_s, copies = _trunk(
        hs, ps, wts, lambda name, layer, part=0, n_parts=1: (big[name], layer, part, n_parts),
        seq_len=None, n_steps=n_steps, states=states)

    hp = x_prompt.reshape(bp * seq_len, d)
    pp = p_prompt.reshape(depth, bp * seq_len, -1)
    yp, a_p, b_p, c_p, f_p, _ = _trunk(
        hp, pp, wts, lambda name, layer, part=0, n_parts=1: (copies[name, layer][part], 0, 0, 1),
        seq_len=seq_len, n_steps=None, states=None)

    tm = min(ROW_TILE, seq_len)
    tps = seq_len // tm

    def prompt_tail(tails, width):
        per_seq = tails.shape[0] // bp
        return tails[per_seq - 1::per_seq, SUBLANES - (width - 1):, :]

    def conformer_tail(tails):
        per_seq = tails.shape[0] // bp
        last = tails[per_seq - 1::per_seq]
        rows = jnp.swapaxes(last, 1, 2).reshape(bp, last.shape[2], d)
        return rows[:, rows.shape[1] - (wa - 1):]

    bmaj = lambda a: jnp.swapaxes(a, 0, 1)
    wa, wb, wf = a_w_dw.shape[1], b_w_dw.shape[1], f_w_dw.shape[1]
    return (
        yp.reshape(bp, seq_len, d),
        bmaj(ys.reshape(n_steps, bs, d)),
        jnp.stack([conformer_tail(t) for t in a_p]),
        jnp.stack([bmaj(s) for s in a_s]),
        jnp.stack([prompt_tail(t, wb) for t in b_p]),
        jnp.stack([bmaj(s) for s in b_s]),
        jnp.stack(c_p),
        jnp.stack([bmaj(v.reshape(n_steps, bs, d)) for v in c_s]),
        jnp.stack([prompt_tail(t, wf) for t in f_p]),
        jnp.stack([bmaj(s) for s in f_s]),
    )
```

```python
import functools

import jax
import jax.numpy as jnp
from jax import lax
from jax.experimental import pallas as pl
from jax.experimental.pallas import tpu as pltpu

EPS = 1e-6
F32 = jnp.float32
BF16 = jnp.bfloat16

ROW_TILE = 512
PROJ_ROW_TILE = 1024
COL_TILE = 512
CONFORMER_ROW_TILE = 256
GMLP_ROW_TILE = 256
SHORT_ROW_TILE = 256
CONV_COL_TILE = 256
SAMPLE_COL_TILE = 256
CONV_ROW_CHUNK = 64
GATE_ROW_CHUNK = 64
SUBLANES = 8
VMEM_LIMIT = 56 * 1024 * 1024


def _params(*sem):
    return pltpu.CompilerParams(dimension_semantics=sem, vmem_limit_bytes=VMEM_LIMIT)


def _rmsnorm(x, g):
    return x * lax.rsqrt(jnp.mean(x * x, axis=-1, keepdims=True) + EPS) * g


def _layernorm(x, g, b):
    xc = x - jnp.mean(x, axis=-1, keepdims=True)
    return xc * lax.rsqrt(jnp.mean(xc * xc, axis=-1, keepdims=True) + EPS) * g + b


def _dot(a, b):
    return jnp.dot(a, b, preferred_element_type=F32)


def _round_up(n, m):
    return -(-n // m) * m


def _weight_spec(operand, k, tn, col_of):
    arr, layer, part, n_parts = operand
    off = part * (arr.shape[2] // n_parts // tn)
    return pl.BlockSpec((None, k, tn), lambda *idx: (layer, 0, col_of(*idx) + off))


def _weight_row_spec(operand, tk, n):
    _, layer, _, n_parts = operand
    assert n_parts == 1
    return pl.BlockSpec((None, tk, n), lambda *idx: (layer, idx[-1], 0))


def _bf16_copy_spec(k, tn, col_of):
    return pl.BlockSpec((None, k, tn), lambda *idx: (0, 0, col_of(*idx)))


def _weight_tile(w_ref, copy_ref):
    if copy_ref is None:
        return w_ref[...]
    wb = w_ref[...].astype(BF16)
    copy_ref[...] = wb
    return wb


def _conv_slabs(slabs, w_ref, n_out):
    width = w_ref.shape[0]
    out = []
    for t in range(n_out):
        acc = None
        for k in range(width):
            term = w_ref[k:k + 1, :] * slabs[t + k]
            acc = term if acc is None else acc + term
        out.append(acc)
    return out


def _in_proj_kernel(h_ref, g_ref, w0_ref, w1_ref, b0_ref, b1_ref, *rest, kind, emit):
    outs, xn_ref = rest[:-1], rest[-1]

    @pl.when(pl.program_id(1) == 0)
    def _():
        xn_ref[...] = _rmsnorm(h_ref[...], g_ref[...]).astype(BF16)

    n_act = 1 if kind == "glu" else 2
    xn = xn_ref[...]
    w0 = _weight_tile(w0_ref, outs[n_act] if emit else None)
    w1 = _weight_tile(w1_ref, outs[n_act + 1] if emit else None)
    a = _dot(xn, w0) + b0_ref[...]
    b = _dot(xn, w1) + b1_ref[...]
    if kind == "glu":
        outs[0][...] = a * jax.nn.sigmoid(b)
    else:
        outs[0][...] = jax.nn.gelu(a)
        outs[1][...] = jax.nn.gelu(b)


def _in_proj(h, g, g_layer, w_parts, b, layer, kind, emit):
    m, d = h.shape
    tm, tn = min(PROJ_ROW_TILE, m), min(COL_TILE, d)
    nj = d // tn
    n_act = 1 if kind == "glu" else 2
    col = lambda i, j: j
    out_specs = [pl.BlockSpec((tm, tn), lambda i, j: (i, j))] * n_act
    out_shape = [jax.ShapeDtypeStruct((m, d), F32)] * n_act
    if emit:
        assert m == tm
        out_specs += [_bf16_copy_spec(d, tn, col)] * 2
        out_shape += [jax.ShapeDtypeStruct((1, d, d), BF16)] * 2
    return pl.pallas_call(
        functools.partial(_in_proj_kernel, kind=kind, emit=emit),
        grid=(m // tm, nj),
        in_specs=[
            pl.BlockSpec((tm, d), lambda i, j: (i, 0)),
            pl.BlockSpec((None, 1, d), lambda i, j: (g_layer, 0, 0)),
            _weight_spec(w_parts[0], d, tn, col),
            _weight_spec(w_parts[1], d, tn, col),
            pl.BlockSpec((None, 1, tn), lambda i, j: (layer, 0, j)),
            pl.BlockSpec((None, 1, tn), lambda i, j: (layer, 0, j + nj)),
        ],
        out_specs=out_specs, out_shape=out_shape,
        scratch_shapes=[pltpu.VMEM((tm, d), BF16)],
        compiler_params=_params("parallel", "arbitrary"),
        name=f"in_proj_{kind}",
    )(h, g, w_parts[0][0], w_parts[1][0], b, b)


def _gate(conv, bias, other, mode):
    if mode == "ffn":
        return jax.nn.silu(conv + bias) * other
    return other * conv


def _gated_prompt_kernel(*refs, n_in, mode, tiles_per_seq, row_chunk):
    h_ref, g_ref = refs[:2]
    w_in = refs[2:2 + n_in]
    wdw_ref, bdw_ref, w2_ref, ho_ref, tail_ref, xn_ref, ext_ref, oth_ref, hid_ref, carry_ref = refs[2 + n_in:]
    i, j = pl.program_id(0), pl.program_id(1)
    tm, tf = oth_ref.shape
    hist = SUBLANES
    width = wdw_ref.shape[0]
    base = hist - (width - 1)

    @pl.when(j == 0)
    def _():
        xn_ref[...] = _rmsnorm(h_ref[...], g_ref[...]).astype(BF16)
        ho_ref[...] = h_ref[...]

        @pl.when(i % tiles_per_seq == 0)
        def _():
            carry_ref[...] = jnp.zeros(carry_ref.shape, F32)

    xn = xn_ref[...]
    ext_ref[0:hist, :] = carry_ref[j]
    ext_ref[hist:hist + tm, :] = _dot(xn, w_in[0][...])
    oth_ref[...] = _dot(xn, w_in[1][...])
    tail = ext_ref[tm:tm + hist, :]
    carry_ref[j] = tail
    tail_ref[...] = tail

    for r0 in range(0, tm, row_chunk):
        for c0 in range(0, tf, 128):
            cs = slice(c0, c0 + 128)
            conv = None
            for k in range(width):
                term = wdw_ref[k:k + 1, cs] * ext_ref[base + k + r0:base + k + r0 + row_chunk, cs]
                conv = term if conv is None else conv + term
            hid = _gate(conv, bdw_ref[:, cs], oth_ref[r0:r0 + row_chunk, cs], mode)
            hid_ref[r0:r0 + row_chunk, cs] = hid.astype(BF16)

    ho_ref[...] += _dot(hid_ref[...], w2_ref[...])


def _gated_sample_kernel(*refs, n_in, mode, n_steps):
    h_ref, g_ref = refs[:2]
    w_in = refs[2:2 + n_in]
    wdw_ref, bdw_ref, w2_ref, st_ref, ho_ref, ns_ref = refs[2 + n_in:8 + n_in]
    copies = refs[8 + n_in:9 + 2 * n_in]
    xn_ref = refs[-1]
    j = pl.program_id(0)
    n_hist = st_ref.shape[0]
    rb = h_ref.shape[0] // n_steps

    @pl.when(j == 0)
    def _():
        xn_ref[...] = _rmsnorm(h_ref[...], g_ref[...]).astype(BF16)
        ho_ref[...] = h_ref[...]

    xn = xn_ref[...]
    proj = [_dot(xn, _weight_tile(w, c)) for w, c in zip(w_in, copies[:n_in])]
    v, other = (proj[0], proj[1]) if mode == "ffn" else (proj[1] * proj[2], proj[0])
    slabs = [st_ref[k] for k in range(n_hist)] + [v[t * rb:(t + 1) * rb, :] for t in range(n_steps)]
    for k in range(n_hist):
        ns_ref[k] = slabs[n_steps + k]
    conv = jnp.concatenate(_conv_slabs(slabs, wdw_ref, n_steps), axis=0)
    hid = _gate(conv, bdw_ref[...], other, mode)
    ho_ref[...] += _dot(hid.astype(BF16), _weight_tile(w2_ref, copies[n_in]))


def _gated(h, g, g_layer, w_in_list, wdw, bdw, w2, layer, mode, *, seq_len=None, state=None, n_steps=None):
    m, d = h.shape
    hidden = wdw.shape[2]
    sample = state is not None
    tf = min(SAMPLE_COL_TILE if sample else COL_TILE, hidden)
    nj = hidden // tf
    width = wdw.shape[1]
    n_in = len(w_in_list)
    tm = m if sample else min(ROW_TILE, seq_len)
    if sample:
        row = lambda j: (0, 0)
        vec = lambda l: (lambda j: (l, 0, 0))
        hid_vec = lambda j: (layer, 0, j)
        grid = (nj,)
    else:
        row = lambda i, j: (i, 0)
        vec = lambda l: (lambda i, j: (l, 0, 0))
        hid_vec = lambda i, j: (layer, 0, j)
        grid = (m // tm, nj)
    col = lambda *idx: idx[-1]
    in_specs = [pl.BlockSpec((tm, d), row), pl.BlockSpec((None, 1, d), vec(g_layer))]
    in_specs += [_weight_spec(w, d, tf, col) for w in w_in_list]
    in_specs += [
        pl.BlockSpec((None, width, tf), hid_vec),
        pl.BlockSpec((None, 1, tf), hid_vec),
        _weight_row_spec(w2, tf, d),
    ]
    args = [h, g] + [w[0] for w in w_in_list] + [wdw, bdw, w2[0]]
    scratch = [pltpu.VMEM((tm, d), BF16)]
    if sample:
        batch = state.shape[2]
        in_specs.append(pl.BlockSpec((None, width - 1, batch, tf), lambda j: (layer, 0, 0, j)))
        args.append(state)
        out_specs = [pl.BlockSpec((tm, d), row), pl.BlockSpec((width - 1, batch, tf), lambda j: (0, 0, j))]
        out_shape = [jax.ShapeDtypeStruct((m, d), F32), jax.ShapeDtypeStruct((width - 1, batch, hidden), F32)]
        out_specs += [_bf16_copy_spec(d, tf, col)] * n_in + [pl.BlockSpec((None, tf, d), lambda j: (0, j, 0))]
        out_shape += [jax.ShapeDtypeStruct((1, d, hidden), BF16)] * n_in + [jax.ShapeDtypeStruct((1, hidden, d), BF16)]
        body = functools.partial(_gated_sample_kernel, n_in=n_in, mode=mode, n_steps=n_steps)
        sem = ("arbitrary",)
    else:
        assert mode == "ffn" and seq_len % tm == 0 and width - 1 <= SUBLANES <= tm
        out_specs = [pl.BlockSpec((tm, d), row), pl.BlockSpec((None, SUBLANES, tf), lambda i, j: (i, 0, j))]
        out_shape = [jax.ShapeDtypeStruct((m, d), F32), jax.ShapeDtypeStruct((m // tm, SUBLANES, hidden), F32)]
        scratch += [pltpu.VMEM((SUBLANES + tm, tf), F32), pltpu.VMEM((tm, tf), F32), pltpu.VMEM((tm, tf), BF16),
                    pltpu.VMEM((nj, SUBLANES, tf), F32)]
        body = functools.partial(_gated_prompt_kernel, n_in=n_in, mode=mode, tiles_per_seq=seq_len // tm,
                                 row_chunk=min(GATE_ROW_CHUNK, tm))
        sem = ("arbitrary", "arbitrary")
    return pl.pallas_call(
        body, grid=grid, in_specs=in_specs, out_specs=out_specs, out_shape=out_shape,
        scratch_shapes=scratch, compiler_params=_params(*sem),
        name=f"gated_{mode}_{'sample' if sample else 'prompt'}",
    )(*args)


def _short_prompt_kernel(h_ref, g_ref, wb_ref, wc_ref, wx_ref, wdw_ref, wo_ref, ho_ref, tail_ref, ext_ref, oth_ref, hid_ref,
                         *, tiles_per_seq, row_chunk):
    i = pl.program_id(0)
    nb, tm, tn = oth_ref.shape
    hist = ext_ref.shape[1] - tm
    width = wdw_ref.shape[0]
    base = hist - (width - 1)
    cols = [slice(b * tn, (b + 1) * tn) for b in range(nb)]
    seq_start = i % tiles_per_seq == 0

    @pl.when(seq_start)
    def _():
        for b in range(nb):
            ext_ref[b, 0:hist, :] = jnp.zeros((hist, tn), F32)

    @pl.when(jnp.logical_not(seq_start))
    def _():
        for b in range(nb):
            ext_ref[b, 0:hist, :] = ext_ref[b, tm:tm + hist, :]

    xn = _rmsnorm(h_ref[...], g_ref[...]).astype(BF16)
    for b in range(nb):
        oth_ref[b] = _dot(xn, wb_ref[:, cols[b]])
        ext_ref[b, hist:hist + tm, :] = _dot(xn, wc_ref[:, cols[b]]) * _dot(xn, wx_ref[:, cols[b]])
        tail_ref[:, cols[b]] = ext_ref[b, tm:tm + hist, :]
        for r0 in range(0, tm, row_chunk):
            for c0 in range(0, tn, 128):
                cs = slice(c0, c0 + 128)
                conv = None
                for k in range(width):
                    term = (wdw_ref[k:k + 1, b * tn + c0:b * tn + c0 + 128]
                            * ext_ref[b, base + k + r0:base + k + r0 + row_chunk, cs])
                    conv = term if conv is None else conv + term
                hid_ref[b, r0:r0 + row_chunk, cs] = (oth_ref[b, r0:r0 + row_chunk, cs] * conv).astype(BF16)

    for c in range(nb):
        acc = h_ref[:, cols[c]]
        for b in range(nb):
            acc = acc + _dot(hid_ref[b], wo_ref[b * tn:(b + 1) * tn, cols[c]])
        ho_ref[:, cols[c]] = acc


def _short_prompt(h, wts, w_in_parts, w_out, g_layer, layer, seq_len):
    m, d = h.shape
    width = wts["b_w_dw"].shape[1]
    tm, tn = min(SHORT_ROW_TILE, seq_len), min(COL_TILE, d)
    nb = d // tn
    hist = SUBLANES
    assert seq_len % tm == 0 and width - 1 <= hist <= tm
    resident = pl.Buffered(1)
    whole = lambda operand: pl.BlockSpec((None, d, d), lambda i: (operand[1], 0, 0), pipeline_mode=resident)
    return pl.pallas_call(
        functools.partial(_short_prompt_kernel, tiles_per_seq=seq_len // tm, row_chunk=min(GATE_ROW_CHUNK, tm)),
        grid=(m // tm,),
        in_specs=[
            pl.BlockSpec((tm, d), lambda i: (i, 0)),
            pl.BlockSpec((None, 1, d), lambda i: (g_layer, 0, 0)),
            whole(w_in_parts[0]), whole(w_in_parts[1]), whole(w_in_parts[2]),
            pl.BlockSpec((None, width, d), lambda i: (layer, 0, 0)),
            whole(w_out),
        ],
        out_specs=[pl.BlockSpec((tm, d), lambda i: (i, 0)), pl.BlockSpec((None, hist, d), lambda i: (i, 0, 0))],
        out_shape=[jax.ShapeDtypeStruct((m, d), F32), jax.ShapeDtypeStruct((m // tm, hist, d), F32)],
        scratch_shapes=[
            pltpu.VMEM((nb, hist + tm, tn), F32),
            pltpu.VMEM((nb, tm, tn), F32),
            pltpu.VMEM((nb, tm, tn), BF16),
        ],
        compiler_params=_params("arbitrary"),
        name="short_prompt",
    )(h, wts["g_mix"], w_in_parts[0][0], w_in_parts[1][0], w_in_parts[2][0], wts["b_w_dw"], w_out[0])


def _conformer_prompt_kernel(h_ref, hprev_ref, g_ref, wa_ref, wg_ref, bin_ref, wdw_ref, bdw_ref, lng_ref, lnb_ref, wo_ref, bo_ref,
                             ho_ref, ztail_ref, zext_ref, sh_ref, y_ref, s_ref, *, n_tiles, tiles_per_seq, row_chunk):
    i = pl.program_id(0)
    nb, tm, tn = y_ref.shape
    hist = zext_ref.shape[1] - tm
    width = wdw_ref.shape[0]
    base = hist - (width - 1)
    d = nb * tn
    cols = [slice(b * tn, (b + 1) * tn) for b in range(nb)]
    seq_start = jnp.minimum(i, n_tiles - 1) % tiles_per_seq == 0

    @pl.when(i == 0)
    def _():
        y_ref[...] = jnp.zeros(y_ref.shape, F32)

    @pl.when(seq_start)
    def _():
        for b in range(nb):
            zext_ref[b, 0:hist, :] = jnp.zeros((hist, tn), F32)

    @pl.when(jnp.logical_not(seq_start))
    def _():
        for b in range(nb):
            zext_ref[b, 0:hist, :] = zext_ref[b, tm:tm + hist, :]

    def in_proj_block(xn, b):
        a = _dot(xn, wa_ref[:, cols[b]]) + bin_ref[:, cols[b]]
        g = _dot(xn, wg_ref[:, cols[b]]) + bin_ref[:, d + b * tn:d + (b + 1) * tn]
        zext_ref[b, hist:hist + tm, :] = a * jax.nn.sigmoid(g)

    def conv_block(b):
        n_shifted = hist + tm - SUBLANES
        for s in range(1, SUBLANES):
            sh_ref[s - 1, 0:n_shifted, :] = zext_ref[b, s:s + n_shifted, :]
        for c0 in range(0, tn, 128):
            cs = slice(c0, c0 + 128)
            w_rows = [wdw_ref[k:k + 1, b * tn + c0:b * tn + c0 + 128] for k in range(width)]
            for r0 in range(0, tm, row_chunk):
                acc = None
                for k in range(width):
                    q, s = divmod(base + k, SUBLANES)
                    rows = slice(r0 + q * SUBLANES, r0 + q * SUBLANES + row_chunk)
                    src = zext_ref[b, rows, cs] if s == 0 else sh_ref[s - 1, rows, cs]
                    term = w_rows[k] * src
                    acc = term if acc is None else acc + term
                y_ref[b, r0:r0 + row_chunk, cs] = acc

    yb = [y_ref[b] + bdw_ref[:, cols[b]] for b in range(nb)]
    mean = sum(jnp.sum(v, axis=-1, keepdims=True) for v in yb) / d
    yc = [v - mean for v in yb]
    var = sum(jnp.sum(v * v, axis=-1, keepdims=True) for v in yc) / d
    rstd = lax.rsqrt(var + EPS)
    for b in range(nb):
        s_ref[b] = jax.nn.silu(yc[b] * rstd * lng_ref[:, cols[b]] + lnb_ref[:, cols[b]]).astype(BF16)

    xn = _rmsnorm(h_ref[...], g_ref[...]).astype(BF16)
    in_proj_block(xn, 0)
    for b in range(1, nb):
        conv_block(b - 1)
        in_proj_block(xn, b)
    conv_block(nb - 1)
    for b in range(nb):
        ztail_ref[b] = zext_ref[b, tm:tm + hist, :]

    for c in range(nb):
        acc = hprev_ref[:, cols[c]] + bo_ref[:, cols[c]]
        for b in range(nb):
            acc = acc + _dot(s_ref[b], wo_ref[b * tn:(b + 1) * tn, cols[c]])
        ho_ref[:, cols[c]] = acc


def _conformer_prompt(h, wts, w_in_parts, w_out, g_layer, layer, seq_len):
    m, d = h.shape
    width = wts["a_w_dw"].shape[1]
    tm, tn = min(CONFORMER_ROW_TILE, seq_len), min(COL_TILE, d)
    nb = d // tn
    n_tiles = m // tm
    hist = _round_up(width - 1, SUBLANES)
    assert seq_len % tm == 0 and hist <= tm
    vec = lambda i: (layer, 0, 0)
    tile = lambda i: (jnp.minimum(i, n_tiles - 1), 0)
    prev = lambda i: (jnp.maximum(i - 1, 0), 0)
    resident = pl.Buffered(1)
    return pl.pallas_call(
        functools.partial(_conformer_prompt_kernel, n_tiles=n_tiles, tiles_per_seq=seq_len // tm,
                          row_chunk=min(CONV_ROW_CHUNK, tm)),
        grid=(n_tiles + 1,),
        in_specs=[
            pl.BlockSpec((tm, d), tile),
            pl.BlockSpec((tm, d), prev),
            pl.BlockSpec((None, 1, d), lambda i: (g_layer, 0, 0)),
            pl.BlockSpec((None, d, d), lambda i: (w_in_parts[0][1], 0, 0), pipeline_mode=resident),
            pl.BlockSpec((None, d, d), lambda i: (w_in_parts[1][1], 0, 0), pipeline_mode=resident),
            pl.BlockSpec((None, 1, 2 * d), vec),
            pl.BlockSpec((None, width, d), vec),
            pl.BlockSpec((None, 1, d), vec),
            pl.BlockSpec((None, 1, d), vec),
            pl.BlockSpec((None, 1, d), vec),
            pl.BlockSpec((None, d, d), lambda i: (w_out[1], 0, 0), pipeline_mode=resident),
            pl.BlockSpec((None, 1, d), vec),
        ],
        out_specs=[
            pl.BlockSpec((tm, d), prev),
            pl.BlockSpec((None, nb, hist, tn), lambda i: (jnp.minimum(i, n_tiles - 1), 0, 0, 0)),
        ],
        out_shape=[jax.ShapeDtypeStruct((m, d), F32), jax.ShapeDtypeStruct((n_tiles, nb, hist, tn), F32)],
        scratch_shapes=[
            pltpu.VMEM((nb, hist + tm, tn), F32),
            pltpu.VMEM((SUBLANES - 1, hist + tm, tn), F32),
            pltpu.VMEM((nb, tm, tn), F32),
            pltpu.VMEM((nb, tm, tn), BF16),
        ],
        compiler_params=_params("arbitrary"),
        name="conformer_prompt",
    )(h, h, wts["g_mix"], w_in_parts[0][0], w_in_parts[1][0], wts["a_b_in"], wts["a_w_dw"], wts["a_b_dw"], wts["a_ln_g"],
      wts["a_ln_b"], w_out[0], wts["a_b_out"])


def _gmlp_prompt_kernel(h_ref, g_ref, wu_ref, wv_ref, bin_ref, lng_ref, lnb_ref, ws_ref, bs_ref, wo_ref, bo_ref,
                        ho_ref, vlast_ref, u_ref, v_ref, s_ref, *, chunk, n_groups):
    nb, tm, tn = u_ref.shape
    d = nb * tn
    gd = d // n_groups
    groups_per_block = tn // gd
    cols = [slice(b * tn, (b + 1) * tn) for b in range(nb)]

    xn = _rmsnorm(h_ref[...], g_ref[...]).astype(BF16)
    for b in range(nb):
        u_ref[b] = jax.nn.gelu(_dot(xn, wu_ref[:, cols[b]]) + bin_ref[:, cols[b]])
        v_ref[b] = jax.nn.gelu(_dot(xn, wv_ref[:, cols[b]]) + bin_ref[:, d + b * tn:d + (b + 1) * tn])

    mean = sum(jnp.sum(v_ref[b], axis=-1, keepdims=True) for b in range(nb)) / d
    var = sum(jnp.sum((v_ref[b] - mean) ** 2, axis=-1, keepdims=True) for b in range(nb)) / d
    rstd = lax.rsqrt(var + EPS)
    rows = lax.broadcasted_iota(jnp.int32, (chunk, chunk), 0)
    lanes = lax.broadcasted_iota(jnp.int32, (chunk, chunk), 1)
    tril = (rows >= lanes).astype(F32)
    for b in range(nb):
        vln = (v_ref[b] - mean) * rstd * lng_ref[:, cols[b]] + lnb_ref[:, cols[b]]
        vlast_ref[:, cols[b]] = vln[tm - chunk:, :]
        vb = vln.astype(BF16)
        for gl in range(groups_per_block):
            g = b * groups_per_block + gl
            wg = (ws_ref[g] * tril).astype(BF16)
            local, glob = slice(gl * gd, (gl + 1) * gd), slice(g * gd, (g + 1) * gd)
            for c in range(tm // chunk):
                rs = slice(c * chunk, (c + 1) * chunk)
                z = _dot(wg, vb[rs, local]) + bs_ref[:, glob]
                s_ref[b, rs, local] = (u_ref[b, rs, local] * z).astype(BF16)

    for c in range(nb):
        acc = h_ref[:, cols[c]] + bo_ref[:, cols[c]]
        for b in range(nb):
            acc = acc + _dot(s_ref[b], wo_ref[b * tn:(b + 1) * tn, cols[c]])
        ho_ref[:, cols[c]] = acc


def _gmlp_prompt(h, wts, w_in_parts, w_out, g_layer, layer, seq_len):
    m, d = h.shape
    n_groups, chunk = wts["c_w_s"].shape[1], wts["c_w_s"].shape[2]
    tm, tn = min(GMLP_ROW_TILE, seq_len), min(COL_TILE, d)
    nb = d // tn
    tps = seq_len // tm
    assert seq_len % tm == 0 and tm % chunk == 0 and tn % (d // n_groups) == 0
    vec = lambda i: (layer, 0, 0)
    resident = pl.Buffered(1)
    return pl.pallas_call(
        functools.partial(_gmlp_prompt_kernel, chunk=chunk, n_groups=n_groups),
        grid=(m // tm,),
        in_specs=[
            pl.BlockSpec((tm, d), lambda i: (i, 0)),
            pl.BlockSpec((None, 1, d), lambda i: (g_layer, 0, 0)),
            pl.BlockSpec((None, d, d), lambda i: (w_in_parts[0][1], 0, 0), pipeline_mode=resident),
            pl.BlockSpec((None, d, d), lambda i: (w_in_parts[1][1], 0, 0), pipeline_mode=resident),
            pl.BlockSpec((None, 1, 2 * d), vec),
            pl.BlockSpec((None, 1, d), vec),
            pl.BlockSpec((None, 1, d), vec),
            pl.BlockSpec((None, n_groups, chunk, chunk), lambda i: (layer, 0, 0, 0)),
            pl.BlockSpec((None, chunk, d), vec),
            pl.BlockSpec((None, d, d), lambda i: (w_out[1], 0, 0), pipeline_mode=resident),
            pl.BlockSpec((None, 1, d), vec),
        ],
        out_specs=[
            pl.BlockSpec((tm, d), lambda i: (i, 0)),
            pl.BlockSpec((None, chunk, d), lambda i: (i // tps, 0, 0)),
        ],
        out_shape=[jax.ShapeDtypeStruct((m, d), F32), jax.ShapeDtypeStruct((m // seq_len, chunk, d), F32)],
        scratch_shapes=[
            pltpu.VMEM((nb, tm, tn), F32),
            pltpu.VMEM((nb, tm, tn), F32),
            pltpu.VMEM((nb, tm, tn), BF16),
        ],
        compiler_params=_params("arbitrary"),
        name="gmlp_prompt",
    )(h, wts["g_mix"], w_in_parts[0][0], w_in_parts[1][0], wts["c_b_in"], wts["c_ln_g"], wts["c_ln_b"],
      wts["c_w_s"], wts["c_bs_rows"], w_out[0], wts["c_b_out"])


def _dwconv_sample_kernel(z_ref, st_ref, w_ref, y_ref, ns_ref, *, n_steps):
    n_hist = st_ref.shape[0]
    rb = z_ref.shape[0] // n_steps
    slabs = [st_ref[k] for k in range(n_hist)] + [z_ref[t * rb:(t + 1) * rb, :] for t in range(n_steps)]
    for k in range(n_hist):
        ns_ref[k] = slabs[n_steps + k]
    y = _conv_slabs(slabs, w_ref, n_steps)
    for t in range(n_steps):
        y_ref[t * rb:(t + 1) * rb, :] = y[t]


def _dwconv_sample(z, state, w, layer, n_steps):
    m, d = z.shape
    width = w.shape[1]
    batch = state.shape[2]
    tc = min(CONV_COL_TILE, d)
    return pl.pallas_call(
        functools.partial(_dwconv_sample_kernel, n_steps=n_steps),
        grid=(d // tc,),
        in_specs=[
            pl.BlockSpec((m, tc), lambda c: (0, c)),
            pl.BlockSpec((None, width - 1, batch, tc), lambda c: (layer, 0, 0, c)),
            pl.BlockSpec((None, width, tc), lambda c: (layer, 0, c)),
        ],
        out_specs=[pl.BlockSpec((m, tc), lambda c: (0, c)), pl.BlockSpec((width - 1, batch, tc), lambda c: (0, 0, c))],
        out_shape=[jax.ShapeDtypeStruct((m, d), F32), jax.ShapeDtypeStruct((width - 1, batch, d), F32)],
        compiler_params=_params("parallel"),
        name="dwconv_sample",
    )(z, state, w)


def _out_proj_kernel(h_ref, w_ref, b_ref, *rest, kind, n_steps, emit):
    j = pl.program_id(1)
    s_ref = rest[-1]
    copy_ref = rest[-2] if emit else None
    rest = rest[:-2] if emit else rest[:-1]
    if kind == "conf":
        y_ref, bdw_ref, lg_ref, lb_ref, o_ref = rest
    else:
        u_ref, v_ref, lg_ref, lb_ref, coef_ref, bs_ref, o_ref, vln_ref = rest

    @pl.when(j == 0)
    def _():
        if kind == "conf":
            s_ref[...] = jax.nn.silu(_layernorm(y_ref[...] + bdw_ref[...], lg_ref[...], lb_ref[...])).astype(BF16)
        else:
            rb = v_ref.shape[0] // n_steps
            vln = _layernorm(v_ref[...], lg_ref[...], lb_ref[...])
            vln_ref[...] = vln
            vb = vln.astype(BF16).astype(F32)
            for t in range(n_steps):
                z = bs_ref[t:t + 1, :]
                for s in range(t + 1):
                    coef = coef_ref[t * n_steps + s:t * n_steps + s + 1, :].astype(BF16).astype(F32)
                    z = z + coef * vb[s * rb:(s + 1) * rb, :]
                s_ref[t * rb:(t + 1) * rb, :] = (u_ref[t * rb:(t + 1) * rb, :] * z).astype(BF16)

    o_ref[...] = h_ref[...] + _dot(s_ref[...], _weight_tile(w_ref, copy_ref)) + b_ref[...]


def _out_proj(h, w, b, layer, kind, tm, ins, in_specs_extra, extra_out=None, *, emit, n_steps=0):
    m, d = h.shape
    tn = min(COL_TILE, d)
    col = lambda i, j: j
    in_specs = [
        pl.BlockSpec((tm, tn), lambda i, j: (i, j)),
        _weight_spec(w, d, tn, col),
        pl.BlockSpec((None, 1, tn), lambda i, j: (layer, 0, j)),
    ] + in_specs_extra
    out_specs = [pl.BlockSpec((tm, tn), lambda i, j: (i, j))]
    out_shape = [jax.ShapeDtypeStruct((m, d), F32)]
    if extra_out is not None:
        out_specs.append(extra_out[0])
        out_shape.append(extra_out[1])
    if emit:
        assert m == tm
        out_specs.append(_bf16_copy_spec(d, tn, col))
        out_shape.append(jax.ShapeDtypeStruct((1, d, d), BF16))
    return pl.pallas_call(
        functools.partial(_out_proj_kernel, kind=kind, n_steps=n_steps, emit=emit),
        grid=(m // tm, d // tn), in_specs=in_specs, out_specs=out_specs, out_shape=out_shape,
        scratch_shapes=[pltpu.VMEM((tm, d), BF16)],
        compiler_params=_params("arbitrary", "arbitrary"),
        name=f"out_proj_{kind}",
    )(h, w[0], b, *ins)


def _ple_kernel(h_ref, g_ref, wg_ref, p_ref, wp_ref, *rest, final, tn):
    if final:
        gf_ref, o_ref, y_ref = rest
    else:
        (o_ref,) = rest
    d = h_ref.shape[1]
    xn = _rmsnorm(h_ref[...], g_ref[...]).astype(BF16)
    pb = p_ref[...].astype(BF16)
    for c0 in range(0, d, tn):
        gate = jax.nn.sigmoid(_dot(xn, wg_ref[:, c0:c0 + tn]))
        o_ref[:, c0:c0 + tn] = h_ref[:, c0:c0 + tn] + gate * _dot(pb, wp_ref[:, c0:c0 + tn])
    if final:
        y_ref[...] = _rmsnorm(o_ref[...], gf_ref[...])


def _ple(h, g, wg, p, wp, layer, g_final=None):
    m, d = h.shape
    pdim = p.shape[2]
    tm = min(ROW_TILE, m)
    final = g_final is not None
    whole = lambda i: 0
    in_specs = [
        pl.BlockSpec((tm, d), lambda i: (i, 0)),
        pl.BlockSpec((None, 1, d), lambda i: (layer, 0, 0)),
        _weight_spec(wg, d, d, whole),
        pl.BlockSpec((None, tm, pdim), lambda i: (layer, i, 0)),
        _weight_spec(wp, pdim, d, whole),
    ]
    args = [h, g, wg[0], p, wp[0]]
    out_specs = [pl.BlockSpec((tm, d), lambda i: (i, 0))]
    out_shape = [jax.ShapeDtypeStruct((m, d), F32)]
    if final:
        in_specs.append(pl.BlockSpec((1, d), lambda i: (0, 0)))
        args.append(g_final)
        out_specs.append(pl.BlockSpec((tm, d), lambda i: (i, 0)))
        out_shape.append(jax.ShapeDtypeStruct((m, d), F32))
    return pl.pallas_call(
        functools.partial(_ple_kernel, final=final, tn=min(COL_TILE, d)),
        grid=(m // tm,), in_specs=in_specs, out_specs=out_specs, out_shape=out_shape,
        compiler_params=_params("parallel"),
        name="ple_final" if final else "ple",
    )(*args)


def _ple_sample_kernel(h_ref, hcol_ref, g_ref, wg_ref, p_ref, wp_ref, o_ref, wg_copy_ref, wp_copy_ref, xn_ref):
    @pl.when(pl.program_id(0) == 0)
    def _():
        xn_ref[...] = _rmsnorm(h_ref[...], g_ref[...]).astype(BF16)

    gate = jax.nn.sigmoid(_dot(xn_ref[...], _weight_tile(wg_ref, wg_copy_ref)))
    o_ref[...] = hcol_ref[...] + gate * _dot(p_ref[...].astype(BF16), _weight_tile(wp_ref, wp_copy_ref))


def _ple_sample(h, g, wg, p, wp, layer):
    m, d = h.shape
    pdim = p.shape[2]
    tn = min(COL_TILE, d)
    col = lambda j: j
    return pl.pallas_call(
        _ple_sample_kernel,
        grid=(d // tn,),
        in_specs=[
            pl.BlockSpec((m, d), lambda j: (0, 0)),
            pl.BlockSpec((m, tn), lambda j: (0, j)),
            pl.BlockSpec((None, 1, d), lambda j: (layer, 0, 0)),
            _weight_spec(wg, d, tn, col),
            pl.BlockSpec((None, m, pdim), lambda j: (layer, 0, 0)),
            _weight_spec(wp, pdim, tn, col),
        ],
        out_specs=[pl.BlockSpec((m, tn), lambda j: (0, j)), _bf16_copy_spec(d, tn, col), _bf16_copy_spec(pdim, tn, col)],
        out_shape=[jax.ShapeDtypeStruct((m, d), F32), jax.ShapeDtypeStruct((1, d, d), BF16),
                   jax.ShapeDtypeStruct((1, pdim, d), BF16)],
        scratch_shapes=[pltpu.VMEM((m, d), BF16)],
        compiler_params=_params("arbitrary"),
        name="ple_sample",
    )(h, h, g, wg[0], p, wp[0])


def _final_norm_kernel(h_ref, g_ref, y_ref):
    y_ref[...] = _rmsnorm(h_ref[...], g_ref[...])


def _final_norm(h, g):
    m, d = h.shape
    tm = min(ROW_TILE, m)
    return pl.pallas_call(
        _final_norm_kernel,
        grid=(m // tm,),
        in_specs=[pl.BlockSpec((tm, d), lambda i: (i, 0)), pl.BlockSpec((1, d), lambda i: (0, 0))],
        out_specs=pl.BlockSpec((tm, d), lambda i: (i, 0)),
        out_shape=jax.ShapeDtypeStruct((m, d), F32),
        compiler_params=_params("parallel"),
        name="final_norm",
    )(h, g)


def _trunk(h, p, wts, weight, *, seq_len, n_steps, states):
    sample = states is not None
    depth = wts["g_mix"].shape[0]
    new_a, new_b, new_c, new_f, copies = [], [], [], [], {}
    for i in range(depth):
        kind, l = i % 3, i // 3
        m, d = h.shape
        vec = lambda i_, j_, l=l: (l, 0, 0)
        if kind == 0 and not sample:
            h, ns = _conformer_prompt(h, wts, [weight("a_w_in", l, 0, 2), weight("a_w_in", l, 1, 2)],
                                      weight("a_w_out", l), i, l, seq_len)
            new_a.append(ns)
        elif kind == 0:
            z, wa_copy, wg_copy = _in_proj(h, wts["g_mix"], i, [weight("a_w_in", l, 0, 2), weight("a_w_in", l, 1, 2)],
                                           wts["a_b_in"], l, "glu", True)
            conv, ns = _dwconv_sample(z, states[0], wts["a_w_dw"], l, n_steps)
            new_a.append(ns)
            h, wo_copy = _out_proj(
                h, weight("a_w_out", l), wts["a_b_out"], l, "conf", m,
                [conv, wts["a_b_dw"], wts["a_ln_g"], wts["a_ln_b"]],
                [pl.BlockSpec((m, d), lambda i_, j_: (0, 0))] + [pl.BlockSpec((None, 1, d), vec)] * 3, emit=True)
            copies["a_w_in", l], copies["a_w_out", l] = (wa_copy, wg_copy), (wo_copy,)
        elif kind == 1 and not sample:
            h, ns = _short_prompt(h, wts, [weight("b_w_in", l, part, 3) for part in range(3)],
                                  weight("b_w_out", l), i, l, seq_len)
            new_b.append(ns)
        elif kind == 1:
            res = _gated(h, wts["g_mix"], i, [weight("b_w_in", l, part, 3) for part in range(3)],
                         wts["b_w_dw"], wts["b_zero_bias"], weight("b_w_out", l), l, "short",
                         state=states[1], n_steps=n_steps)
            h, ns = res[:2]
            new_b.append(ns)
            copies["b_w_in", l], copies["b_w_out", l] = tuple(res[2:5]), (res[5],)
        elif not sample:
            h, nv = _gmlp_prompt(h, wts, [weight("c_w_in", l, 0, 2), weight("c_w_in", l, 1, 2)],
                                 weight("c_w_out", l), i, l, seq_len)
            new_c.append(nv)
        else:
            u, v, wu_copy, wv_copy = _in_proj(h, wts["g_mix"], i, [weight("c_w_in", l, 0, 2), weight("c_w_in", l, 1, 2)],
                                              wts["c_b_in"], l, "gelu2", True)
            full = pl.BlockSpec((m, d), lambda i_, j_: (0, 0))
            h, nv, wo_copy = _out_proj(
                h, weight("c_w_out", l), wts["c_b_out"], l, "gmlp_sample", m,
                [u, v, wts["c_ln_g"], wts["c_ln_b"], wts["c_coef"], wts["c_bs_steps"]],
                [full, full] + [pl.BlockSpec((None, 1, d), vec)] * 2
                + [pl.BlockSpec((None, n_steps * n_steps, d), vec), pl.BlockSpec((None, n_steps, d), vec)],
                (full, jax.ShapeDtypeStruct((m, d), F32)), emit=True, n_steps=n_steps)
            copies["c_w_in", l], copies["c_w_out", l] = (wu_copy, wv_copy), (wo_copy,)
            new_c.append(nv)
        res = _gated(h, wts["g_ffn"], i, [weight("f_w_gate", i), weight("f_w_up", i)],
                     wts["f_w_dw"], wts["f_b_dw"], weight("f_w_down", i), i, "ffn",
                     seq_len=seq_len, state=states[2] if sample else None, n_steps=n_steps)
        h, nf = res[:2]
        new_f.append(nf)
        if sample:
            copies["f_w_gate", i], copies["f_w_up", i], copies["f_w_down", i] = (res[2],), (res[3],), (res[4],)
            h, wg_copy, wp_copy = _ple_sample(h, wts["g_ple"], weight("ple_w_gate", i), p, weight("ple_w_proj", i), i)
            copies["ple_w_gate", i], copies["ple_w_proj", i] = (wg_copy,), (wp_copy,)
            if i == depth - 1:
                y = _final_norm(h, wts["g_final"])
        elif i == depth - 1:
            h, y = _ple(h, wts["g_ple"], weight("ple_w_gate", i), p, weight("ple_w_proj", i), i, wts["g_final"])
        else:
            (h,) = _ple(h, wts["g_ple"], weight("ple_w_gate", i), p, weight("ple_w_proj", i), i)
    return y, new_a, new_b, new_c, new_f, copies


def kernel(x_prompt, x_sample, p_prompt, p_sample, state_conformer, state_shortconv, state_ffn, g_mix, g_ffn, g_ple, g_final, a_w_in, a_b_in, a_w_dw, a_b_dw, a_ln_g, a_ln_b, a_w_out, a_b_out, b_w_in, b_w_dw, b_w_out, c_w_in, c_b_in, c_ln_g, c_ln_b, c_w_s, c_b_s, c_w_out, c_b_out, f_w_gate, f_w_dw, f_b_dw, f_w_up, f_w_down, ple_w_proj, ple_w_gate):
    bp, seq_len, d = x_prompt.shape
    bs, n_steps, _ = x_sample.shape
    depth = g_mix.shape[0]
    n_groups, chunk = c_w_s.shape[1], c_w_s.shape[2]
    gd = d // n_groups
    assert n_steps <= chunk

    row3 = lambda a: a[:, None, :]
    wts = dict(
        g_mix=row3(g_mix), g_ffn=row3(g_ffn), g_ple=row3(g_ple), g_final=g_final[None, :],
        a_b_in=row3(a_b_in), a_w_dw=a_w_dw, a_b_dw=row3(a_b_dw),
        a_ln_g=row3(a_ln_g), a_ln_b=row3(a_ln_b), a_b_out=row3(a_b_out),
        b_w_dw=b_w_dw, b_zero_bias=jnp.zeros((b_w_dw.shape[0], 1, d), F32),
        c_b_in=row3(c_b_in), c_ln_g=row3(c_ln_g), c_ln_b=row3(c_ln_b), c_w_s=c_w_s, c_b_out=row3(c_b_out),
        c_bs_rows=jnp.repeat(jnp.swapaxes(c_b_s, 1, 2), gd, axis=2),
        c_bs_steps=jnp.repeat(jnp.swapaxes(c_b_s, 1, 2)[:, :n_steps], gd, axis=2),
        c_coef=jnp.repeat(jnp.transpose(c_w_s[:, :, :n_steps, :n_steps], (0, 2, 3, 1)).reshape(
            c_w_s.shape[0], n_steps * n_steps, n_groups), gd, axis=2),
        f_w_dw=f_w_dw, f_b_dw=row3(f_b_dw),
    )
    big = dict(a_w_in=a_w_in, a_w_out=a_w_out, b_w_in=b_w_in, b_w_out=b_w_out, c_w_in=c_w_in, c_w_out=c_w_out,
               f_w_gate=f_w_gate, f_w_up=f_w_up, f_w_down=f_w_down, ple_w_gate=ple_w_gate, ple_w_proj=ple_w_proj)

    tmaj = lambda a: jnp.swapaxes(a, 1, 2)
    hs = jnp.swapaxes(x_sample, 0, 1).reshape(n_steps * bs, d)
    ps = tmaj(p_sample).reshape(depth, n_steps * bs, -1)
    states = (tmaj(state_conformer), tmaj(state_shortconv), tmaj(state_ffn))
    ys, a_s, b_s, c_s, f_s, copies = _trunk(
        hs, ps, wts, lambda name, layer, part=0, n_parts=1: (big[name], layer, part, n_parts),
        seq_len=None, n_steps=n_steps, states=states)

    hp = x_prompt.reshape(bp * seq_len, d)
    pp = p_prompt.reshape(depth, bp * seq_len, -1)
    yp, a_p, b_p, c_p, f_p, _ = _trunk(
        hp, pp, wts, lambda name, layer, part=0, n_parts=1: (copies[name, layer][part], 0, 0, 1),
        seq_len=seq_len, n_steps=None, states=None)

    tm = min(ROW_TILE, seq_len)
    tps = seq_len // tm

    def prompt_tail(tails, width):
        per_seq = tails.shape[0] // bp
        return tails[per_seq - 1::per_seq, SUBLANES - (width - 1):, :]

    def conformer_tail(tails):
        per_seq = tails.shape[0] // bp
        last = tails[per_seq - 1::per_seq]
        rows = jnp.swapaxes(last, 1, 2).reshape(bp, last.shape[2], d)
        return rows[:, rows.shape[1] - (wa - 1):]

    bmaj = lambda a: jnp.swapaxes(a, 0, 1)
    wa, wb, wf = a_w_dw.shape[1], b_w_dw.shape[1], f_w_dw.shape[1]
    return (
        yp.reshape(bp, seq_len, d),
        bmaj(ys.reshape(n_steps, bs, d)),
        jnp.stack([conformer_tail(t) for t in a_p]),
        jnp.stack([bmaj(s) for s in a_s]),
        jnp.stack([prompt_tail(t, wb) for t in b_p]),
        jnp.stack([bmaj(s) for s in b_s]),
        jnp.stack(c_p),
        jnp.stack([bmaj(v.reshape(n_steps, bs, d)) for v in c_s]),
        jnp.stack([prompt_tail(t, wf) for t in f_p]),
        jnp.stack([bmaj(s) for s in f_s]),
    )
```

```python
import functools

import jax
import jax.numpy as jnp
from jax import lax
from jax.experimental import pallas as pl
from jax.experimental.pallas import tpu as pltpu

EPS = 1e-6
F32 = jnp.float32
BF16 = jnp.bfloat16

ROW_TILE = 512
PROJ_ROW_TILE = 1024
COL_TILE = 512
CONFORMER_ROW_TILE = 256
GMLP_ROW_TILE = 256
SHORT_ROW_TILE = 256
FFN_PARTS = 2
FFN_PART_ROW_TILE = 256
FFN_PART_COL_TILE = 256
FFN_PART_VMEM_LIMIT = 60 * 1024 * 1024
CONV_COL_TILE = 256
SAMPLE_COL_TILE = 256
CONV_ROW_CHUNK = 64
GATE_ROW_CHUNK = 64
SUBLANES = 8
VMEM_LIMIT = 56 * 1024 * 1024


def _params(*sem):
    return pltpu.CompilerParams(dimension_semantics=sem, vmem_limit_bytes=VMEM_LIMIT)


def _rmsnorm(x, g):
    return x * lax.rsqrt(jnp.mean(x * x, axis=-1, keepdims=True) + EPS) * g


def _layernorm(x, g, b):
    xc = x - jnp.mean(x, axis=-1, keepdims=True)
    return xc * lax.rsqrt(jnp.mean(xc * xc, axis=-1, keepdims=True) + EPS) * g + b


def _dot(a, b):
    return jnp.dot(a, b, preferred_element_type=F32)


def _round_up(n, m):
    return -(-n // m) * m


def _weight_spec(operand, k, tn, col_of):
    arr, layer, part, n_parts = operand
    off = part * (arr.shape[2] // n_parts // tn)
    return pl.BlockSpec((None, k, tn), lambda *idx: (layer, 0, col_of(*idx) + off))


def _weight_row_spec(operand, tk, n):
    _, layer, _, n_parts = operand
    assert n_parts == 1
    return pl.BlockSpec((None, tk, n), lambda *idx: (layer, idx[-1], 0))


def _bf16_copy_spec(k, tn, col_of):
    return pl.BlockSpec((None, k, tn), lambda *idx: (0, 0, col_of(*idx)))


def _weight_tile(w_ref, copy_ref):
    if copy_ref is None:
        return w_ref[...]
    wb = w_ref[...].astype(BF16)
    copy_ref[...] = wb
    return wb


def _conv_slabs(slabs, w_ref, n_out):
    width = w_ref.shape[0]
    out = []
    for t in range(n_out):
        acc = None
        for k in range(width):
            term = w_ref[k:k + 1, :] * slabs[t + k]
            acc = term if acc is None else acc + term
        out.append(acc)
    return out


def _in_proj_kernel(h_ref, g_ref, w0_ref, w1_ref, b0_ref, b1_ref, *rest, kind, emit):
    outs, xn_ref = rest[:-1], rest[-1]

    @pl.when(pl.program_id(1) == 0)
    def _():
        xn_ref[...] = _rmsnorm(h_ref[...], g_ref[...]).astype(BF16)

    n_act = 1 if kind == "glu" else 2
    xn = xn_ref[...]
    w0 = _weight_tile(w0_ref, outs[n_act] if emit else None)
    w1 = _weight_tile(w1_ref, outs[n_act + 1] if emit else None)
    a = _dot(xn, w0) + b0_ref[...]
    b = _dot(xn, w1) + b1_ref[...]
    if kind == "glu":
        outs[0][...] = a * jax.nn.sigmoid(b)
    else:
        outs[0][...] = jax.nn.gelu(a)
        outs[1][...] = jax.nn.gelu(b)


def _in_proj(h, g, g_layer, w_parts, b, layer, kind, emit):
    m, d = h.shape
    tm, tn = min(PROJ_ROW_TILE, m), min(COL_TILE, d)
    nj = d // tn
    n_act = 1 if kind == "glu" else 2
    col = lambda i, j: j
    out_specs = [pl.BlockSpec((tm, tn), lambda i, j: (i, j))] * n_act
    out_shape = [jax.ShapeDtypeStruct((m, d), F32)] * n_act
    if emit:
        assert m == tm
        out_specs += [_bf16_copy_spec(d, tn, col)] * 2
        out_shape += [jax.ShapeDtypeStruct((1, d, d), BF16)] * 2
    return pl.pallas_call(
        functools.partial(_in_proj_kernel, kind=kind, emit=emit),
        grid=(m // tm, nj),
        in_specs=[
            pl.BlockSpec((tm, d), lambda i, j: (i, 0)),
            pl.BlockSpec((None, 1, d), lambda i, j: (g_layer, 0, 0)),
            _weight_spec(w_parts[0], d, tn, col),
            _weight_spec(w_parts[1], d, tn, col),
            pl.BlockSpec((None, 1, tn), lambda i, j: (layer, 0, j)),
            pl.BlockSpec((None, 1, tn), lambda i, j: (layer, 0, j + nj)),
        ],
        out_specs=out_specs, out_shape=out_shape,
        scratch_shapes=[pltpu.VMEM((tm, d), BF16)],
        compiler_params=_params("parallel", "arbitrary"),
        name=f"in_proj_{kind}",
    )(h, g, w_parts[0][0], w_parts[1][0], b, b)


def _gate(conv, bias, other, mode):
    if mode == "ffn":
        return jax.nn.silu(conv + bias) * other
    return other * conv


def _gated_prompt_kernel(*refs, n_in, mode, tiles_per_seq, row_chunk):
    h_ref, g_ref = refs[:2]
    w_in = refs[2:2 + n_in]
    wdw_ref, bdw_ref, w2_ref, ho_ref, tail_ref, xn_ref, ext_ref, oth_ref, hid_ref, carry_ref = refs[2 + n_in:]
    i, j = pl.program_id(0), pl.program_id(1)
    tm, tf = oth_ref.shape
    hist = SUBLANES
    width = wdw_ref.shape[0]
    base = hist - (width - 1)

    @pl.when(j == 0)
    def _():
        xn_ref[...] = _rmsnorm(h_ref[...], g_ref[...]).astype(BF16)
        ho_ref[...] = h_ref[...]

        @pl.when(i % tiles_per_seq == 0)
        def _():
            carry_ref[...] = jnp.zeros(carry_ref.shape, F32)

    xn = xn_ref[...]
    ext_ref[0:hist, :] = carry_ref[j]
    ext_ref[hist:hist + tm, :] = _dot(xn, w_in[0][...])
    oth_ref[...] = _dot(xn, w_in[1][...])
    tail = ext_ref[tm:tm + hist, :]
    carry_ref[j] = tail
    tail_ref[...] = tail

    for r0 in range(0, tm, row_chunk):
        for c0 in range(0, tf, 128):
            cs = slice(c0, c0 + 128)
            conv = None
            for k in range(width):
                term = wdw_ref[k:k + 1, cs] * ext_ref[base + k + r0:base + k + r0 + row_chunk, cs]
                conv = term if conv is None else conv + term
            hid = _gate(conv, bdw_ref[:, cs], oth_ref[r0:r0 + row_chunk, cs], mode)
            hid_ref[r0:r0 + row_chunk, cs] = hid.astype(BF16)

    ho_ref[...] += _dot(hid_ref[...], w2_ref[...])


def _gated_sample_kernel(*refs, n_in, mode, n_steps):
    h_ref, g_ref = refs[:2]
    w_in = refs[2:2 + n_in]
    wdw_ref, bdw_ref, w2_ref, st_ref, ho_ref, ns_ref = refs[2 + n_in:8 + n_in]
    copies = refs[8 + n_in:9 + 2 * n_in]
    xn_ref = refs[-1]
    j = pl.program_id(0)
    n_hist = st_ref.shape[0]
    rb = h_ref.shape[0] // n_steps

    @pl.when(j == 0)
    def _():
        xn_ref[...] = _rmsnorm(h_ref[...], g_ref[...]).astype(BF16)
        ho_ref[...] = h_ref[...]

    xn = xn_ref[...]
    proj = [_dot(xn, _weight_tile(w, c)) for w, c in zip(w_in, copies[:n_in])]
    v, other = (proj[0], proj[1]) if mode == "ffn" else (proj[1] * proj[2], proj[0])
    slabs = [st_ref[k] for k in range(n_hist)] + [v[t * rb:(t + 1) * rb, :] for t in range(n_steps)]
    for k in range(n_hist):
        ns_ref[k] = slabs[n_steps + k]
    conv = jnp.concatenate(_conv_slabs(slabs, wdw_ref, n_steps), axis=0)
    hid = _gate(conv, bdw_ref[...], other, mode)
    ho_ref[...] += _dot(hid.astype(BF16), _weight_tile(w2_ref, copies[n_in]))


def _gated(h, g, g_layer, w_in_list, wdw, bdw, w2, layer, mode, *, seq_len=None, state=None, n_steps=None):
    m, d = h.shape
    hidden = wdw.shape[2]
    sample = state is not None
    tf = min(SAMPLE_COL_TILE if sample else COL_TILE, hidden)
    nj = hidden // tf
    width = wdw.shape[1]
    n_in = len(w_in_list)
    tm = m if sample else min(ROW_TILE, seq_len)
    if sample:
        row = lambda j: (0, 0)
        vec = lambda l: (lambda j: (l, 0, 0))
        hid_vec = lambda j: (layer, 0, j)
        grid = (nj,)
    else:
        row = lambda i, j: (i, 0)
        vec = lambda l: (lambda i, j: (l, 0, 0))
        hid_vec = lambda i, j: (layer, 0, j)
        grid = (m // tm, nj)
    col = lambda *idx: idx[-1]
    in_specs = [pl.BlockSpec((tm, d), row), pl.BlockSpec((None, 1, d), vec(g_layer))]
    in_specs += [_weight_spec(w, d, tf, col) for w in w_in_list]
    in_specs += [
        pl.BlockSpec((None, width, tf), hid_vec),
        pl.BlockSpec((None, 1, tf), hid_vec),
        _weight_row_spec(w2, tf, d),
    ]
    args = [h, g] + [w[0] for w in w_in_list] + [wdw, bdw, w2[0]]
    scratch = [pltpu.VMEM((tm, d), BF16)]
    if sample:
        batch = state.shape[2]
        in_specs.append(pl.BlockSpec((None, width - 1, batch, tf), lambda j: (layer, 0, 0, j)))
        args.append(state)
        out_specs = [pl.BlockSpec((tm, d), row), pl.BlockSpec((width - 1, batch, tf), lambda j: (0, 0, j))]
        out_shape = [jax.ShapeDtypeStruct((m, d), F32), jax.ShapeDtypeStruct((width - 1, batch, hidden), F32)]
        out_specs += [_bf16_copy_spec(d, tf, col)] * n_in + [pl.BlockSpec((None, tf, d), lambda j: (0, j, 0))]
        out_shape += [jax.ShapeDtypeStruct((1, d, hidden), BF16)] * n_in + [jax.ShapeDtypeStruct((1, hidden, d), BF16)]
        body = functools.partial(_gated_sample_kernel, n_in=n_in, mode=mode, n_steps=n_steps)
        sem = ("arbitrary",)
    else:
        assert mode == "ffn" and seq_len % tm == 0 and width - 1 <= SUBLANES <= tm
        out_specs = [pl.BlockSpec((tm, d), row), pl.BlockSpec((None, SUBLANES, tf), lambda i, j: (i, 0, j))]
        out_shape = [jax.ShapeDtypeStruct((m, d), F32), jax.ShapeDtypeStruct((m // tm, SUBLANES, hidden), F32)]
        scratch += [pltpu.VMEM((SUBLANES + tm, tf), F32), pltpu.VMEM((tm, tf), F32), pltpu.VMEM((tm, tf), BF16),
                    pltpu.VMEM((nj, SUBLANES, tf), F32)]
        body = functools.partial(_gated_prompt_kernel, n_in=n_in, mode=mode, tiles_per_seq=seq_len // tm,
                                 row_chunk=min(GATE_ROW_CHUNK, tm))
        sem = ("arbitrary", "arbitrary")
    return pl.pallas_call(
        body, grid=grid, in_specs=in_specs, out_specs=out_specs, out_shape=out_shape,
        scratch_shapes=scratch, compiler_params=_params(*sem),
        name=f"gated_{mode}_{'sample' if sample else 'prompt'}",
    )(*args)


def _ffn_part_kernel(*refs, has_acc, tiles_per_seq, row_chunk, out_tile):
    h_ref = refs[0]
    acc_ref = refs[1] if has_acc else h_ref
    g_ref, wg_ref, wu_ref, wdw_ref, bdw_ref, wd_ref, ho_ref, tail_ref, ext_ref, oth_ref, hid_ref = refs[1 + has_acc:]
    i = pl.program_id(0)
    nb, tm, tb = oth_ref.shape
    hist = ext_ref.shape[1] - tm
    width = wdw_ref.shape[0]
    base = hist - (width - 1)
    d = h_ref.shape[1]
    seq_start = i % tiles_per_seq == 0

    @pl.when(seq_start)
    def _():
        for b in range(nb):
            ext_ref[b, 0:hist, :] = jnp.zeros((hist, tb), F32)

    @pl.when(jnp.logical_not(seq_start))
    def _():
        for b in range(nb):
            ext_ref[b, 0:hist, :] = ext_ref[b, tm:tm + hist, :]

    xn = _rmsnorm(h_ref[...], g_ref[...]).astype(BF16)
    for b in range(nb):
        cb = slice(b * tb, (b + 1) * tb)
        ext_ref[b, hist:hist + tm, :] = _dot(xn, wg_ref[:, cb])
        oth_ref[b] = _dot(xn, wu_ref[:, cb])
        tail_ref[:, cb] = ext_ref[b, tm:tm + hist, :]
        for r0 in range(0, tm, row_chunk):
            for c0 in range(0, tb, 128):
                cs = slice(c0, c0 + 128)
                gs = slice(b * tb + c0, b * tb + c0 + 128)
                conv = None
                for k in range(width):
                    term = wdw_ref[k:k + 1, gs] * ext_ref[b, base + k + r0:base + k + r0 + row_chunk, cs]
                    conv = term if conv is None else conv + term
                hid = jax.nn.silu(conv + bdw_ref[:, gs]) * oth_ref[b, r0:r0 + row_chunk, cs]
                hid_ref[b, r0:r0 + row_chunk, cs] = hid.astype(BF16)

    for c0 in range(0, d, out_tile):
        cs = slice(c0, c0 + out_tile)
        acc = acc_ref[:, cs]
        for b in range(nb):
            acc = acc + _dot(hid_ref[b], wd_ref[b * tb:(b + 1) * tb, cs])
        ho_ref[:, cs] = acc


def _ffn_part_prompt(h, acc, wts, w_gate, w_up, w_down, g_layer, layer, part, n_parts, seq_len):
    m, d = h.shape
    width, hidden = wts["f_w_dw"].shape[1], wts["f_w_dw"].shape[2]
    hp = hidden // n_parts
    tm, tb = min(FFN_PART_ROW_TILE, seq_len), min(FFN_PART_COL_TILE, hp)
    hist = SUBLANES
    assert hidden % n_parts == 0 and hp % tb == 0 and seq_len % tm == 0 and width - 1 <= hist <= tm
    resident = pl.Buffered(1)
    rows = pl.BlockSpec((tm, d), lambda i: (i, 0))
    in_specs = [rows] + ([rows] if acc is not None else []) + [
        pl.BlockSpec((None, 1, d), lambda i: (g_layer, 0, 0)),
        pl.BlockSpec((None, d, hp), lambda i: (w_gate[1], 0, part), pipeline_mode=resident),
        pl.BlockSpec((None, d, hp), lambda i: (w_up[1], 0, part), pipeline_mode=resident),
        pl.BlockSpec((None, width, hp), lambda i: (layer, 0, part)),
        pl.BlockSpec((None, 1, hp), lambda i: (layer, 0, part)),
        pl.BlockSpec((None, hp, d), lambda i: (w_down[1], part, 0), pipeline_mode=resident),
    ]
    args = [h] + ([acc] if acc is not None else []) + [wts["g_ffn"], w_gate[0], w_up[0], wts["f_w_dw"], wts["f_b_dw"], w_down[0]]
    return pl.pallas_call(
        functools.partial(_ffn_part_kernel, has_acc=acc is not None, tiles_per_seq=seq_len // tm,
                          row_chunk=min(GATE_ROW_CHUNK, tm), out_tile=min(COL_TILE, d)),
        grid=(m // tm,),
        in_specs=in_specs,
        out_specs=[rows, pl.BlockSpec((None, hist, hp), lambda i: (i, 0, 0))],
        out_shape=[jax.ShapeDtypeStruct((m, d), F32), jax.ShapeDtypeStruct((m // tm, hist, hp), F32)],
        scratch_shapes=[
            pltpu.VMEM((hp // tb, hist + tm, tb), F32),
            pltpu.VMEM((hp // tb, tm, tb), F32),
            pltpu.VMEM((hp // tb, tm, tb), BF16),
        ],
        compiler_params=pltpu.CompilerParams(dimension_semantics=("arbitrary",), vmem_limit_bytes=FFN_PART_VMEM_LIMIT),
        name="ffn_part_prompt",
    )(*args)


def _short_prompt_kernel(h_ref, g_ref, wb_ref, wc_ref, wx_ref, wdw_ref, wo_ref, ho_ref, tail_ref, ext_ref, oth_ref, hid_ref,
                         *, tiles_per_seq, row_chunk):
    i = pl.program_id(0)
    nb, tm, tn = oth_ref.shape
    hist = ext_ref.shape[1] - tm
    width = wdw_ref.shape[0]
    base = hist - (width - 1)
    cols = [slice(b * tn, (b + 1) * tn) for b in range(nb)]
    seq_start = i % tiles_per_seq == 0

    @pl.when(seq_start)
    def _():
        for b in range(nb):
            ext_ref[b, 0:hist, :] = jnp.zeros((hist, tn), F32)

    @pl.when(jnp.logical_not(seq_start))
    def _():
        for b in range(nb):
            ext_ref[b, 0:hist, :] = ext_ref[b, tm:tm + hist, :]

    xn = _rmsnorm(h_ref[...], g_ref[...]).astype(BF16)
    for b in range(nb):
        oth_ref[b] = _dot(xn, wb_ref[:, cols[b]])
        ext_ref[b, hist:hist + tm, :] = _dot(xn, wc_ref[:, cols[b]]) * _dot(xn, wx_ref[:, cols[b]])
        tail_ref[:, cols[b]] = ext_ref[b, tm:tm + hist, :]
        for r0 in range(0, tm, row_chunk):
            for c0 in range(0, tn, 128):
                cs = slice(c0, c0 + 128)
                conv = None
                for k in range(width):
                    term = (wdw_ref[k:k + 1, b * tn + c0:b * tn + c0 + 128]
                            * ext_ref[b, base + k + r0:base + k + r0 + row_chunk, cs])
                    conv = term if conv is None else conv + term
                hid_ref[b, r0:r0 + row_chunk, cs] = (oth_ref[b, r0:r0 + row_chunk, cs] * conv).astype(BF16)

    for c in range(nb):
        acc = h_ref[:, cols[c]]
        for b in range(nb):
            acc = acc + _dot(hid_ref[b], wo_ref[b * tn:(b + 1) * tn, cols[c]])
        ho_ref[:, cols[c]] = acc


def _short_prompt(h, wts, w_in_parts, w_out, g_layer, layer, seq_len):
    m, d = h.shape
    width = wts["b_w_dw"].shape[1]
    tm, tn = min(SHORT_ROW_TILE, seq_len), min(COL_TILE, d)
    nb = d // tn
    hist = SUBLANES
    assert seq_len % tm == 0 and width - 1 <= hist <= tm
    resident = pl.Buffered(1)
    whole = lambda operand: pl.BlockSpec((None, d, d), lambda i: (operand[1], 0, 0), pipeline_mode=resident)
    return pl.pallas_call(
        functools.partial(_short_prompt_kernel, tiles_per_seq=seq_len // tm, row_chunk=min(GATE_ROW_CHUNK, tm)),
        grid=(m // tm,),
        in_specs=[
            pl.BlockSpec((tm, d), lambda i: (i, 0)),
            pl.BlockSpec((None, 1, d), lambda i: (g_layer, 0, 0)),
            whole(w_in_parts[0]), whole(w_in_parts[1]), whole(w_in_parts[2]),
            pl.BlockSpec((None, width, d), lambda i: (layer, 0, 0)),
            whole(w_out),
        ],
        out_specs=[pl.BlockSpec((tm, d), lambda i: (i, 0)), pl.BlockSpec((None, hist, d), lambda i: (i, 0, 0))],
        out_shape=[jax.ShapeDtypeStruct((m, d), F32), jax.ShapeDtypeStruct((m // tm, hist, d), F32)],
        scratch_shapes=[
            pltpu.VMEM((nb, hist + tm, tn), F32),
            pltpu.VMEM((nb, tm, tn), F32),
            pltpu.VMEM((nb, tm, tn), BF16),
        ],
        compiler_params=_params("arbitrary"),
        name="short_prompt",
    )(h, wts["g_mix"], w_in_parts[0][0], w_in_parts[1][0], w_in_parts[2][0], wts["b_w_dw"], w_out[0])


def _conformer_prompt_kernel(h_ref, hprev_ref, g_ref, wa_ref, wg_ref, bin_ref, wdw_ref, bdw_ref, lng_ref, lnb_ref, wo_ref, bo_ref,
                             ho_ref, ztail_ref, zext_ref, sh_ref, y_ref, s_ref, *, n_tiles, tiles_per_seq, row_chunk):
    i = pl.program_id(0)
    nb, tm, tn = y_ref.shape
    hist = zext_ref.shape[1] - tm
    width = wdw_ref.shape[0]
    base = hist - (width - 1)
    d = nb * tn
    cols = [slice(b * tn, (b + 1) * tn) for b in range(nb)]
    seq_start = jnp.minimum(i, n_tiles - 1) % tiles_per_seq == 0

    @pl.when(i == 0)
    def _():
        y_ref[...] = jnp.zeros(y_ref.shape, F32)

    @pl.when(seq_start)
    def _():
        for b in range(nb):
            zext_ref[b, 0:hist, :] = jnp.zeros((hist, tn), F32)

    @pl.when(jnp.logical_not(seq_start))
    def _():
        for b in range(nb):
            zext_ref[b, 0:hist, :] = zext_ref[b, tm:tm + hist, :]

    def in_proj_block(xn, b):
        a = _dot(xn, wa_ref[:, cols[b]]) + bin_ref[:, cols[b]]
        g = _dot(xn, wg_ref[:, cols[b]]) + bin_ref[:, d + b * tn:d + (b + 1) * tn]
        zext_ref[b, hist:hist + tm, :] = a * jax.nn.sigmoid(g)

    def conv_block(b):
        n_shifted = hist + tm - SUBLANES
        for s in range(1, SUBLANES):
            sh_ref[s - 1, 0:n_shifted, :] = zext_ref[b, s:s + n_shifted, :]
        for c0 in range(0, tn, 128):
            cs = slice(c0, c0 + 128)
            w_rows = [wdw_ref[k:k + 1, b * tn + c0:b * tn + c0 + 128] for k in range(width)]
            for r0 in range(0, tm, row_chunk):
                acc = None
                for k in range(width):
                    q, s = divmod(base + k, SUBLANES)
                    rows = slice(r0 + q * SUBLANES, r0 + q * SUBLANES + row_chunk)
                    src = zext_ref[b, rows, cs] if s == 0 else sh_ref[s - 1, rows, cs]
                    term = w_rows[k] * src
                    acc = term if acc is None else acc + term
                y_ref[b, r0:r0 + row_chunk, cs] = acc

    yb = [y_ref[b] + bdw_ref[:, cols[b]] for b in range(nb)]
    mean = sum(jnp.sum(v, axis=-1, keepdims=True) for v in yb) / d
    yc = [v - mean for v in yb]
    var = sum(jnp.sum(v * v, axis=-1, keepdims=True) for v in yc) / d
    rstd = lax.rsqrt(var + EPS)
    for b in range(nb):
        s_ref[b] = jax.nn.silu(yc[b] * rstd * lng_ref[:, cols[b]] + lnb_ref[:, cols[b]]).astype(BF16)

    xn = _rmsnorm(h_ref[...], g_ref[...]).astype(BF16)
    in_proj_block(xn, 0)
    for b in range(1, nb):
        conv_block(b - 1)
        in_proj_block(xn, b)
    conv_block(nb - 1)
    for b in range(nb):
        ztail_ref[b] = zext_ref[b, tm:tm + hist, :]

    for c in range(nb):
        acc = hprev_ref[:, cols[c]] + bo_ref[:, cols[c]]
        for b in range(nb):
            acc = acc + _dot(s_ref[b], wo_ref[b * tn:(b + 1) * tn, cols[c]])
        ho_ref[:, cols[c]] = acc


def _conformer_prompt(h, wts, w_in_parts, w_out, g_layer, layer, seq_len):
    m, d = h.shape
    width = wts["a_w_dw"].shape[1]
    tm, tn = min(CONFORMER_ROW_TILE, seq_len), min(COL_TILE, d)
    nb = d // tn
    n_tiles = m // tm
    hist = _round_up(width - 1, SUBLANES)
    assert seq_len % tm == 0 and hist <= tm
    vec = lambda i: (layer, 0, 0)
    tile = lambda i: (jnp.minimum(i, n_tiles - 1), 0)
    prev = lambda i: (jnp.maximum(i - 1, 0), 0)
    resident = pl.Buffered(1)
    return pl.pallas_call(
        functools.partial(_conformer_prompt_kernel, n_tiles=n_tiles, tiles_per_seq=seq_len // tm,
                          row_chunk=min(CONV_ROW_CHUNK, tm)),
        grid=(n_tiles + 1,),
        in_specs=[
            pl.BlockSpec((tm, d), tile),
            pl.BlockSpec((tm, d), prev),
            pl.BlockSpec((None, 1, d), lambda i: (g_layer, 0, 0)),
            pl.BlockSpec((None, d, d), lambda i: (w_in_parts[0][1], 0, 0), pipeline_mode=resident),
            pl.BlockSpec((None, d, d), lambda i: (w_in_parts[1][1], 0, 0), pipeline_mode=resident),
            pl.BlockSpec((None, 1, 2 * d), vec),
            pl.BlockSpec((None, width, d), vec),
            pl.BlockSpec((None, 1, d), vec),
            pl.BlockSpec((None, 1, d), vec),
            pl.BlockSpec((None, 1, d), vec),
            pl.BlockSpec((None, d, d), lambda i: (w_out[1], 0, 0), pipeline_mode=resident),
            pl.BlockSpec((None, 1, d), vec),
        ],
        out_specs=[
            pl.BlockSpec((tm, d), prev),
            pl.BlockSpec((None, nb, hist, tn), lambda i: (jnp.minimum(i, n_tiles - 1), 0, 0, 0)),
        ],
        out_shape=[jax.ShapeDtypeStruct((m, d), F32), jax.ShapeDtypeStruct((n_tiles, nb, hist, tn), F32)],
        scratch_shapes=[
            pltpu.VMEM((nb, hist + tm, tn), F32),
            pltpu.VMEM((SUBLANES - 1, hist + tm, tn), F32),
            pltpu.VMEM((nb, tm, tn), F32),
            pltpu.VMEM((nb, tm, tn), BF16),
        ],
        compiler_params=_params("arbitrary"),
        name="conformer_prompt",
    )(h, h, wts["g_mix"], w_in_parts[0][0], w_in_parts[1][0], wts["a_b_in"], wts["a_w_dw"], wts["a_b_dw"], wts["a_ln_g"],
      wts["a_ln_b"], w_out[0], wts["a_b_out"])


def _gmlp_prompt_kernel(h_ref, g_ref, wu_ref, wv_ref, bin_ref, lng_ref, lnb_ref, ws_ref, bs_ref, wo_ref, bo_ref,
                        ho_ref, vlast_ref, u_ref, v_ref, s_ref, *, chunk, n_groups):
    nb, tm, tn = u_ref.shape
    d = nb * tn
    gd = d // n_groups
    groups_per_block = tn // gd
    cols = [slice(b * tn, (b + 1) * tn) for b in range(nb)]

    xn = _rmsnorm(h_ref[...], g_ref[...]).astype(BF16)
    for b in range(nb):
        u_ref[b] = jax.nn.gelu(_dot(xn, wu_ref[:, cols[b]]) + bin_ref[:, cols[b]])
        v_ref[b] = jax.nn.gelu(_dot(xn, wv_ref[:, cols[b]]) + bin_ref[:, d + b * tn:d + (b + 1) * tn])

    mean = sum(jnp.sum(v_ref[b], axis=-1, keepdims=True) for b in range(nb)) / d
    var = sum(jnp.sum((v_ref[b] - mean) ** 2, axis=-1, keepdims=True) for b in range(nb)) / d
    rstd = lax.rsqrt(var + EPS)
    rows = lax.broadcasted_iota(jnp.int32, (chunk, chunk), 0)
    lanes = lax.broadcasted_iota(jnp.int32, (chunk, chunk), 1)
    tril = (rows >= lanes).astype(F32)
    for b in range(nb):
        vln = (v_ref[b] - mean) * rstd * lng_ref[:, cols[b]] + lnb_ref[:, cols[b]]
        vlast_ref[:, cols[b]] = vln[tm - chunk:, :]
        vb = vln.astype(BF16)
        for gl in range(groups_per_block):
            g = b * groups_per_block + gl
            wg = (ws_ref[g] * tril).astype(BF16)
            local, glob = slice(gl * gd, (gl + 1) * gd), slice(g * gd, (g + 1) * gd)
            for c in range(tm // chunk):
                rs = slice(c * chunk, (c + 1) * chunk)
                z = _dot(wg, vb[rs, local]) + bs_ref[:, glob]
                s_ref[b, rs, local] = (u_ref[b, rs, local] * z).astype(BF16)

    for c in range(nb):
        acc = h_ref[:, cols[c]] + bo_ref[:, cols[c]]
        for b in range(nb):
            acc = acc + _dot(s_ref[b], wo_ref[b * tn:(b + 1) * tn, cols[c]])
        ho_ref[:, cols[c]] = acc


def _gmlp_prompt(h, wts, w_in_parts, w_out, g_layer, layer, seq_len):
    m, d = h.shape
    n_groups, chunk = wts["c_w_s"].shape[1], wts["c_w_s"].shape[2]
    tm, tn = min(GMLP_ROW_TILE, seq_len), min(COL_TILE, d)
    nb = d // tn
    tps = seq_len // tm
    assert seq_len % tm == 0 and tm % chunk == 0 and tn % (d // n_groups) == 0
    vec = lambda i: (layer, 0, 0)
    resident = pl.Buffered(1)
    return pl.pallas_call(
        functools.partial(_gmlp_prompt_kernel, chunk=chunk, n_groups=n_groups),
        grid=(m // tm,),
        in_specs=[
            pl.BlockSpec((tm, d), lambda i: (i, 0)),
            pl.BlockSpec((None, 1, d), lambda i: (g_layer, 0, 0)),
            pl.BlockSpec((None, d, d), lambda i: (w_in_parts[0][1], 0, 0), pipeline_mode=resident),
            pl.BlockSpec((None, d, d), lambda i: (w_in_parts[1][1], 0, 0), pipeline_mode=resident),
            pl.BlockSpec((None, 1, 2 * d), vec),
            pl.BlockSpec((None, 1, d), vec),
            pl.BlockSpec((None, 1, d), vec),
            pl.BlockSpec((None, n_groups, chunk, chunk), lambda i: (layer, 0, 0, 0)),
            pl.BlockSpec((None, chunk, d), vec),
            pl.BlockSpec((None, d, d), lambda i: (w_out[1], 0, 0), pipeline_mode=resident),
            pl.BlockSpec((None, 1, d), vec),
        ],
        out_specs=[
            pl.BlockSpec((tm, d), lambda i: (i, 0)),
            pl.BlockSpec((None, chunk, d), lambda i: (i // tps, 0, 0)),
        ],
        out_shape=[jax.ShapeDtypeStruct((m, d), F32), jax.ShapeDtypeStruct((m // seq_len, chunk, d), F32)],
        scratch_shapes=[
            pltpu.VMEM((nb, tm, tn), F32),
            pltpu.VMEM((nb, tm, tn), F32),
            pltpu.VMEM((nb, tm, tn), BF16),
        ],
        compiler_params=_params("arbitrary"),
        name="gmlp_prompt",
    )(h, wts["g_mix"], w_in_parts[0][0], w_in_parts[1][0], wts["c_b_in"], wts["c_ln_g"], wts["c_ln_b"],
      wts["c_w_s"], wts["c_bs_rows"], w_out[0], wts["c_b_out"])


def _dwconv_sample_kernel(z_ref, st_ref, w_ref, y_ref, ns_ref, *, n_steps):
    n_hist = st_ref.shape[0]
    rb = z_ref.shape[0] // n_steps
    slabs = [st_ref[k] for k in range(n_hist)] + [z_ref[t * rb:(t + 1) * rb, :] for t in range(n_steps)]
    for k in range(n_hist):
        ns_ref[k] = slabs[n_steps + k]
    y = _conv_slabs(slabs, w_ref, n_steps)
    for t in range(n_steps):
        y_ref[t * rb:(t + 1) * rb, :] = y[t]


def _dwconv_sample(z, state, w, layer, n_steps):
    m, d = z.shape
    width = w.shape[1]
    batch = state.shape[2]
    tc = min(CONV_COL_TILE, d)
    return pl.pallas_call(
        functools.partial(_dwconv_sample_kernel, n_steps=n_steps),
        grid=(d // tc,),
        in_specs=[
            pl.BlockSpec((m, tc), lambda c: (0, c)),
            pl.BlockSpec((None, width - 1, batch, tc), lambda c: (layer, 0, 0, c)),
            pl.BlockSpec((None, width, tc), lambda c: (layer, 0, c)),
        ],
        out_specs=[pl.BlockSpec((m, tc), lambda c: (0, c)), pl.BlockSpec((width - 1, batch, tc), lambda c: (0, 0, c))],
        out_shape=[jax.ShapeDtypeStruct((m, d), F32), jax.ShapeDtypeStruct((width - 1, batch, d), F32)],
        compiler_params=_params("parallel"),
        name="dwconv_sample",
    )(z, state, w)


def _out_proj_kernel(h_ref, w_ref, b_ref, *rest, kind, n_steps, emit):
    j = pl.program_id(1)
    s_ref = rest[-1]
    copy_ref = rest[-2] if emit else None
    rest = rest[:-2] if emit else rest[:-1]
    if kind == "conf":
        y_ref, bdw_ref, lg_ref, lb_ref, o_ref = rest
    else:
        u_ref, v_ref, lg_ref, lb_ref, coef_ref, bs_ref, o_ref, vln_ref = rest

    @pl.when(j == 0)
    def _():
        if kind == "conf":
            s_ref[...] = jax.nn.silu(_layernorm(y_ref[...] + bdw_ref[...], lg_ref[...], lb_ref[...])).astype(BF16)
        else:
            rb = v_ref.shape[0] // n_steps
            vln = _layernorm(v_ref[...], lg_ref[...], lb_ref[...])
            vln_ref[...] = vln
            vb = vln.astype(BF16).astype(F32)
            for t in range(n_steps):
                z = bs_ref[t:t + 1, :]
                for s in range(t + 1):
                    coef = coef_ref[t * n_steps + s:t * n_steps + s + 1, :].astype(BF16).astype(F32)
                    z = z + coef * vb[s * rb:(s + 1) * rb, :]
                s_ref[t * rb:(t + 1) * rb, :] = (u_ref[t * rb:(t + 1) * rb, :] * z).astype(BF16)

    o_ref[...] = h_ref[...] + _dot(s_ref[...], _weight_tile(w_ref, copy_ref)) + b_ref[...]


def _out_proj(h, w, b, layer, kind, tm, ins, in_specs_extra, extra_out=None, *, emit, n_steps=0):
    m, d = h.shape
    tn = min(COL_TILE, d)
    col = lambda i, j: j
    in_specs = [
        pl.BlockSpec((tm, tn), lambda i, j: (i, j)),
        _weight_spec(w, d, tn, col),
        pl.BlockSpec((None, 1, tn), lambda i, j: (layer, 0, j)),
    ] + in_specs_extra
    out_specs = [pl.BlockSpec((tm, tn), lambda i, j: (i, j))]
    out_shape = [jax.ShapeDtypeStruct((m, d), F32)]
    if extra_out is not None:
        out_specs.append(extra_out[0])
        out_shape.append(extra_out[1])
    if emit:
        assert m == tm
        out_specs.append(_bf16_copy_spec(d, tn, col))
        out_shape.append(jax.ShapeDtypeStruct((1, d, d), BF16))
    return pl.pallas_call(
        functools.partial(_out_proj_kernel, kind=kind, n_steps=n_steps, emit=emit),
        grid=(m // tm, d // tn), in_specs=in_specs, out_specs=out_specs, out_shape=out_shape,
        scratch_shapes=[pltpu.VMEM((tm, d), BF16)],
        compiler_params=_params("arbitrary", "arbitrary"),
        name=f"out_proj_{kind}",
    )(h, w[0], b, *ins)


def _ple_kernel(h_ref, g_ref, wg_ref, p_ref, wp_ref, *rest, final, tn):
    if final:
        gf_ref, o_ref, y_ref = rest
    else:
        (o_ref,) = rest
    d = h_ref.shape[1]
    xn = _rmsnorm(h_ref[...], g_ref[...]).astype(BF16)
    pb = p_ref[...].astype(BF16)
    for c0 in range(0, d, tn):
        gate = jax.nn.sigmoid(_dot(xn, wg_ref[:, c0:c0 + tn]))
        o_ref[:, c0:c0 + tn] = h_ref[:, c0:c0 + tn] + gate * _dot(pb, wp_ref[:, c0:c0 + tn])
    if final:
        y_ref[...] = _rmsnorm(o_ref[...], gf_ref[...])


def _ple(h, g, wg, p, wp, layer, g_final=None):
    m, d = h.shape
    pdim = p.shape[2]
    tm = min(ROW_TILE, m)
    final = g_final is not None
    whole = lambda i: 0
    in_specs = [
        pl.BlockSpec((tm, d), lambda i: (i, 0)),
        pl.BlockSpec((None, 1, d), lambda i: (layer, 0, 0)),
        _weight_spec(wg, d, d, whole),
        pl.BlockSpec((None, tm, pdim), lambda i: (layer, i, 0)),
        _weight_spec(wp, pdim, d, whole),
    ]
    args = [h, g, wg[0], p, wp[0]]
    out_specs = [pl.BlockSpec((tm, d), lambda i: (i, 0))]
    out_shape = [jax.ShapeDtypeStruct((m, d), F32)]
    if final:
        in_specs.append(pl.BlockSpec((1, d), lambda i: (0, 0)))
        args.append(g_final)
        out_specs.append(pl.BlockSpec((tm, d), lambda i: (i, 0)))
        out_shape.append(jax.ShapeDtypeStruct((m, d), F32))
    return pl.pallas_call(
        functools.partial(_ple_kernel, final=final, tn=min(COL_TILE, d)),
        grid=(m // tm,), in_specs=in_specs, out_specs=out_specs, out_shape=out_shape,
        compiler_params=_params("parallel"),
        name="ple_final" if final else "ple",
    )(*args)


def _ple_sample_kernel(h_ref, hcol_ref, g_ref, wg_ref, p_ref, wp_ref, o_ref, wg_copy_ref, wp_copy_ref, xn_ref):
    @pl.when(pl.program_id(0) == 0)
    def _():
        xn_ref[...] = _rmsnorm(h_ref[...], g_ref[...]).astype(BF16)

    gate = jax.nn.sigmoid(_dot(xn_ref[...], _weight_tile(wg_ref, wg_copy_ref)))
    o_ref[...] = hcol_ref[...] + gate * _dot(p_ref[...].astype(BF16), _weight_tile(wp_ref, wp_copy_ref))


def _ple_sample(h, g, wg, p, wp, layer):
    m, d = h.shape
    pdim = p.shape[2]
    tn = min(COL_TILE, d)
    col = lambda j: j
    return pl.pallas_call(
        _ple_sample_kernel,
        grid=(d // tn,),
        in_specs=[
            pl.BlockSpec((m, d), lambda j: (0, 0)),
            pl.BlockSpec((m, tn), lambda j: (0, j)),
            pl.BlockSpec((None, 1, d), lambda j: (layer, 0, 0)),
            _weight_spec(wg, d, tn, col),
            pl.BlockSpec((None, m, pdim), lambda j: (layer, 0, 0)),
            _weight_spec(wp, pdim, tn, col),
        ],
        out_specs=[pl.BlockSpec((m, tn), lambda j: (0, j)), _bf16_copy_spec(d, tn, col), _bf16_copy_spec(pdim, tn, col)],
        out_shape=[jax.ShapeDtypeStruct((m, d), F32), jax.ShapeDtypeStruct((1, d, d), BF16),
                   jax.ShapeDtypeStruct((1, pdim, d), BF16)],
        scratch_shapes=[pltpu.VMEM((m, d), BF16)],
        compiler_params=_params("arbitrary"),
        name="ple_sample",
    )(h, h, g, wg[0], p, wp[0])


def _final_norm_kernel(h_ref, g_ref, y_ref):
    y_ref[...] = _rmsnorm(h_ref[...], g_ref[...])


def _final_norm(h, g):
    m, d = h.shape
    tm = min(ROW_TILE, m)
    return pl.pallas_call(
        _final_norm_kernel,
        grid=(m // tm,),
        in_specs=[pl.BlockSpec((tm, d), lambda i: (i, 0)), pl.BlockSpec((1, d), lambda i: (0, 0))],
        out_specs=pl.BlockSpec((tm, d), lambda i: (i, 0)),
        out_shape=jax.ShapeDtypeStruct((m, d), F32),
        compiler_params=_params("parallel"),
        name="final_norm",
    )(h, g)


def _trunk(h, p, wts, weight, *, seq_len, n_steps, states):
    sample = states is not None
    depth = wts["g_mix"].shape[0]
    new_a, new_b, new_c, new_f, copies = [], [], [], [], {}
    for i in range(depth):
        kind, l = i % 3, i // 3
        m, d = h.shape
        vec = lambda i_, j_, l=l: (l, 0, 0)
        if kind == 0 and not sample:
            h, ns = _conformer_prompt(h, wts, [weight("a_w_in", l, 0, 2), weight("a_w_in", l, 1, 2)],
                                      weight("a_w_out", l), i, l, seq_len)
            new_a.append(ns)
        elif kind == 0:
            z, wa_copy, wg_copy = _in_proj(h, wts["g_mix"], i, [weight("a_w_in", l, 0, 2), weight("a_w_in", l, 1, 2)],
                                           wts["a_b_in"], l, "glu", True)
            conv, ns = _dwconv_sample(z, states[0], wts["a_w_dw"], l, n_steps)
            new_a.append(ns)
            h, wo_copy = _out_proj(
                h, weight("a_w_out", l), wts["a_b_out"], l, "conf", m,
                [conv, wts["a_b_dw"], wts["a_ln_g"], wts["a_ln_b"]],
                [pl.BlockSpec((m, d), lambda i_, j_: (0, 0))] + [pl.BlockSpec((None, 1, d), vec)] * 3, emit=True)
            copies["a_w_in", l], copies["a_w_out", l] = (wa_copy, wg_copy), (wo_copy,)
        elif kind == 1 and not sample:
            h, ns = _short_prompt(h, wts, [weight("b_w_in", l, part, 3) for part in range(3)],
                                  weight("b_w_out", l), i, l, seq_len)
            new_b.append(ns)
        elif kind == 1:
            res = _gated(h, wts["g_mix"], i, [weight("b_w_in", l, part, 3) for part in range(3)],
                         wts["b_w_dw"], wts["b_zero_bias"], weight("b_w_out", l), l, "short",
                         state=states[1], n_steps=n_steps)
            h, ns = res[:2]
            new_b.append(ns)
            copies["b_w_in", l], copies["b_w_out", l] = tuple(res[2:5]), (res[5],)
        elif not sample:
            h, nv = _gmlp_prompt(h, wts, [weight("c_w_in", l, 0, 2), weight("c_w_in", l, 1, 2)],
                                 weight("c_w_out", l), i, l, seq_len)
            new_c.append(nv)
        else:
            u, v, wu_copy, wv_copy = _in_proj(h, wts["g_mix"], i, [weight("c_w_in", l, 0, 2), weight("c_w_in", l, 1, 2)],
                                              wts["c_b_in"], l, "gelu2", True)
            full = pl.BlockSpec((m, d), lambda i_, j_: (0, 0))
            h, nv, wo_copy = _out_proj(
                h, weight("c_w_out", l), wts["c_b_out"], l, "gmlp_sample", m,
                [u, v, wts["c_ln_g"], wts["c_ln_b"], wts["c_coef"], wts["c_bs_steps"]],
                [full, full] + [pl.BlockSpec((None, 1, d), vec)] * 2
                + [pl.BlockSpec((None, n_steps * n_steps, d), vec), pl.BlockSpec((None, n_steps, d), vec)],
                (full, jax.ShapeDtypeStruct((m, d), F32)), emit=True, n_steps=n_steps)
            copies["c_w_in", l], copies["c_w_out", l] = (wu_copy, wv_copy), (wo_copy,)
            new_c.append(nv)
        if sample:
            res = _gated(h, wts["g_ffn"], i, [weight("f_w_gate", i), weight("f_w_up", i)],
                         wts["f_w_dw"], wts["f_b_dw"], weight("f_w_down", i), i, "ffn",
                         state=states[2], n_steps=n_steps)
            h, nf = res[:2]
        else:
            acc, tails = None, []
            for part in range(FFN_PARTS):
                acc, tail = _ffn_part_prompt(h, acc, wts, weight("f_w_gate", i), weight("f_w_up", i),
                                             weight("f_w_down", i), i, i, part, FFN_PARTS, seq_len)
                tails.append(tail)
            h, nf = acc, jnp.concatenate(tails, axis=2)
        new_f.append(nf)
        if sample:
            copies["f_w_gate", i], copies["f_w_up", i], copies["f_w_down", i] = (res[2],), (res[3],), (res[4],)
            h, wg_copy, wp_copy = _ple_sample(h, wts["g_ple"], weight("ple_w_gate", i), p, weight("ple_w_proj", i), i)
            copies["ple_w_gate", i], copies["ple_w_proj", i] = (wg_copy,), (wp_copy,)
            if i == depth - 1:
                y = _final_norm(h, wts["g_final"])
        elif i == depth - 1:
            h, y = _ple(h, wts["g_ple"], weight("ple_w_gate", i), p, weight("ple_w_proj", i), i, wts["g_final"])
        else:
            (h,) = _ple(h, wts["g_ple"], weight("ple_w_gate", i), p, weight("ple_w_proj", i), i)
    return y, new_a, new_b, new_c, new_f, copies


def kernel(x_prompt, x_sample, p_prompt, p_sample, state_conformer, state_shortconv, state_ffn, g_mix, g_ffn, g_ple, g_final, a_w_in, a_b_in, a_w_dw, a_b_dw, a_ln_g, a_ln_b, a_w_out, a_b_out, b_w_in, b_w_dw, b_w_out, c_w_in, c_b_in, c_ln_g, c_ln_b, c_w_s, c_b_s, c_w_out, c_b_out, f_w_gate, f_w_dw, f_b_dw, f_w_up, f_w_down, ple_w_proj, ple_w_gate):
    bp, seq_len, d = x_prompt.shape
    bs, n_steps, _ = x_sample.shape
    depth = g_mix.shape[0]
    n_groups, chunk = c_w_s.shape[1], c_w_s.shape[2]
    gd = d // n_groups
    assert n_steps <= chunk

    row3 = lambda a: a[:, None, :]
    wts = dict(
        g_mix=row3(g_mix), g_ffn=row3(g_ffn), g_ple=row3(g_ple), g_final=g_final[None, :],
        a_b_in=row3(a_b_in), a_w_dw=a_w_dw, a_b_dw=row3(a_b_dw),
        a_ln_g=row3(a_ln_g), a_ln_b=row3(a_ln_b), a_b_out=row3(a_b_out),
        b_w_dw=b_w_dw, b_zero_bias=jnp.zeros((b_w_dw.shape[0], 1, d), F32),
        c_b_in=row3(c_b_in), c_ln_g=row3(c_ln_g), c_ln_b=row3(c_ln_b), c_w_s=c_w_s, c_b_out=row3(c_b_out),
        c_bs_rows=jnp.repeat(jnp.swapaxes(c_b_s, 1, 2), gd, axis=2),
        c_bs_steps=jnp.repeat(jnp.swapaxes(c_b_s, 1, 2)[:, :n_steps], gd, axis=2),
        c_coef=jnp.repeat(jnp.transpose(c_w_s[:, :, :n_steps, :n_steps], (0, 2, 3, 1)).reshape(
            c_w_s.shape[0], n_steps * n_steps, n_groups), gd, axis=2),
        f_w_dw=f_w_dw, f_b_dw=row3(f_b_dw),
    )
    big = dict(a_w_in=a_w_in, a_w_out=a_w_out, b_w_in=b_w_in, b_w_out=b_w_out, c_w_in=c_w_in, c_w_out=c_w_out,
               f_w_gate=f_w_gate, f_w_up=f_w_up, f_w_down=f_w_down, ple_w_gate=ple_w_gate, ple_w_proj=ple_w_proj)

    tmaj = lambda a: jnp.swapaxes(a, 1, 2)
    hs = jnp.swapaxes(x_sample, 0, 1).reshape(n_steps * bs, d)
    ps = tmaj(p_sample).reshape(depth, n_steps * bs, -1)
    states = (tmaj(state_conformer), tmaj(state_shortconv), tmaj(state_ffn))
    ys, a_s, b_s, c_s, f_s, copies = _trunk(
        hs, ps, wts, lambda name, layer, part=0, n_parts=1: (big[name], layer, part, n_parts),
        seq_len=None, n_steps=n_steps, states=states)

    hp = x_prompt.reshape(bp * seq_len, d)
    pp = p_prompt.reshape(depth, bp * seq_len, -1)
    yp, a_p, b_p, c_p, f_p, _ = _trunk(
        hp, pp, wts, lambda name, layer, part=0, n_parts=1: (copies[name, layer][part], 0, 0, 1),
        seq_len=seq_len, n_steps=None, states=None)

    tm = min(ROW_TILE, seq_len)
    tps = seq_len // tm

    def prompt_tail(tails, width):
        per_seq = tails.shape[0] // bp
        return tails[per_seq - 1::per_seq, SUBLANES - (width - 1):, :]

    def conformer_tail(tails):
        per_seq = tails.shape[0] // bp
        last = tails[per_seq - 1::per_seq]
        rows = jnp.swapaxes(last, 1, 2).reshape(bp, last.shape[2], d)
        return rows[:, rows.shape[1] - (wa - 1):]

    bmaj = lambda a: jnp.swapaxes(a, 0, 1)
    wa, wb, wf = a_w_dw.shape[1], b_w_dw.shape[1], f_w_dw.shape[1]
    return (
        yp.reshape(bp, seq_len, d),
        bmaj(ys.reshape(n_steps, bs, d)),
        jnp.stack([conformer_tail(t) for t in a_p]),
        jnp.stack([bmaj(s) for s in a_s]),
        jnp.stack([prompt_tail(t, wb) for t in b_p]),
        jnp.stack([bmaj(s) for s in b_s]),
        jnp.stack(c_p),
        jnp.stack([bmaj(v.reshape(n_steps, bs, d)) for v in c_s]),
        jnp.stack([prompt_tail(t, wf) for t in f_p]),
        jnp.stack([bmaj(s) for s in f_s]),
    )
```
